```python
import math
import jax
import jax.numpy as jnp
from jax import lax
import numpy as np

D_MODEL = 1024
BATCH = 1
SEQ = 16384
DEPTH = 1
DEC_BATCH = 32
DEC_SEQ = 4
PAST_LEN = 16384
PAGE_SIZE = 128

GLA_HEADS = 4
GLA_DK = 64
GLA_DV = 128
GLA_GATE_RANK = 16
GLA_GATE_TAU = 16.0
GLA_CHUNK = 64
SB_HEADS = 8
SB_HD = 64
SB_Q_BLOCK = 128
SB_BIAS_INIT = -6.0
MIX_WIDTH = GLA_HEADS * GLA_DV + SB_HEADS * SB_HD
N_MEM = 256
X_HEADS = 4
X_HD = D_MODEL // X_HEADS
D_FF = 2752
FFN_RES = 0.5
NORM_EPS = 1e-6
IN_SIZES = (GLA_HEADS * GLA_DK, GLA_HEADS * GLA_DK, GLA_HEADS * GLA_DV, GLA_GATE_RANK,
            GLA_HEADS * GLA_DV, SB_HEADS * SB_HD, SB_HEADS * SB_HD, SB_HEADS * SB_HD)
IN_WIDTH = sum(IN_SIZES)

kernel_name = 'hymba_gla_stickbreak_macaron_memxattn_step'


def rmsnorm(x, w):
    xf = x.astype(jnp.float32)
    y = xf * lax.rsqrt(jnp.mean(xf * xf, axis=-1, keepdims=True) + NORM_EPS)
    return (y * w.astype(jnp.float32)).astype(x.dtype)


def swiglu(h, w_gate, w_up, w_down):
    return (jax.nn.silu(h @ w_gate) * (h @ w_up)) @ w_down


def gla_chunked(q, k, v, log_a, s0):
    b_sz, t_len, n_h, _ = q.shape
    dv = v.shape[-1]
    c = math.gcd(t_len, GLA_CHUNK)
    n_c = t_len // c

    def to_chunks(a):
        return a.reshape(b_sz, n_c, c, n_h, a.shape[-1]).transpose(1, 0, 2, 3, 4)

    causal = jnp.tril(jnp.ones((c, c), dtype=bool))[None, :, :, None, None]

    def step(s, inp):
        qc, kc, vc, gc = inp
        b = jnp.cumsum(gc, axis=1)
        o_inter = jnp.einsum('bthk,bhkv->bthv', qc * jnp.exp(b), s)
        rel = jnp.exp(jnp.where(causal, b[:, :, None] - b[:, None, :], -jnp.inf))
        att = jnp.einsum('bthk,btshk,bshk->bhts', qc, rel, kc)
        o_intra = jnp.einsum('bhts,bshv->bthv', att, vc)
        b_end = b[:, -1]
        k_dec = kc * jnp.exp(b_end[:, None] - b)
        s_new = jnp.exp(b_end)[..., None] * s + jnp.einsum('bshk,bshv->bhkv', k_dec, vc)
        return s_new, o_inter + o_intra

    s_fin, o = lax.scan(step, s0, (to_chunks(q), to_chunks(k), to_chunks(v), to_chunks(log_a)))
    o = o.transpose(1, 0, 2, 3, 4).reshape(b_sz, t_len, n_h, dv)
    return o, s_fin


def sb_block(q, k, v, q_pos, k_pos, bias):
    z = jnp.einsum('bqhd,bkhd->bhqk', q, k) * (SB_HD ** -0.5) + bias[None, :, None, None]
    mask = (k_pos[None, :] < q_pos[:, None])[None, None]
    log_beta = jax.nn.log_sigmoid(z)
    log_keep = jnp.where(mask, jax.nn.log_sigmoid(-z), 0.0)
    later = lax.cumsum(log_keep, axis=3, reverse=True) - log_keep
    w = jnp.exp(jnp.where(mask, log_beta + later, -jnp.inf))
    return jnp.einsum('bhqk,bkhd->bqhd', w, v)


def sb_attend(q, k, v, past_len, bias):
    b_sz, t_len, n_h, hd = q.shape
    qb_len = math.gcd(t_len, SB_Q_BLOCK)
    n_b = t_len // qb_len
    k_pos = jnp.arange(k.shape[1], dtype=jnp.int32)
    q_pos = (past_len + jnp.arange(t_len, dtype=jnp.int32)).reshape(n_b, qb_len)
    q_blocks = q.astype(jnp.float32).reshape(b_sz, n_b, qb_len, n_h, hd).transpose(1, 0, 2, 3, 4)
    kf = k.astype(jnp.float32)
    vf = v.astype(jnp.float32)
    bf = bias.astype(jnp.float32)
    o = lax.map(lambda a: sb_block(a[0], kf, vf, a[1], k_pos, bf), (q_blocks, q_pos))
    return o.transpose(1, 0, 2, 3, 4).reshape(b_sz, t_len, n_h, hd)


def token_mix(h, past_k, past_v, s0, w_in, w_gate2, b_gate2, gla_norm_w, sb_bias, w_out):
    b_sz, t_len, _ = h.shape
    f32 = jnp.float32
    split_idx = np.cumsum(IN_SIZES)[:-1].tolist()
    q_g, k_g, v_g, g_lr, r_g, q_s, k_s, v_s = jnp.split(h @ w_in, split_idx, axis=-1)
    q_g = q_g.astype(f32).reshape(b_sz, t_len, GLA_HEADS, GLA_DK) * (GLA_DK ** -0.5)
    k_g = k_g.astype(f32).reshape(b_sz, t_len, GLA_HEADS, GLA_DK)
    v_g = v_g.astype(f32).reshape(b_sz, t_len, GLA_HEADS, GLA_DV)
    log_a = (jax.nn.log_sigmoid((g_lr @ w_gate2 + b_gate2).astype(f32)) / GLA_GATE_TAU).reshape(
        b_sz, t_len, GLA_HEADS, GLA_DK)
    o_g, s_new = gla_chunked(q_g, k_g, v_g, log_a, s0.astype(f32))
    o_g = o_g * lax.rsqrt(jnp.mean(o_g * o_g, axis=-1, keepdims=True) + NORM_EPS) * gla_norm_w.astype(f32)
    o_g = o_g.reshape(b_sz, t_len, GLA_HEADS * GLA_DV) * jax.nn.silu(r_g.astype(f32))
    q_s = q_s.reshape(b_sz, t_len, SB_HEADS, SB_HD)
    k_s = k_s.reshape(b_sz, t_len, SB_HEADS, SB_HD)
    v_s = v_s.reshape(b_sz, t_len, SB_HEADS, SB_HD)
    k_all = jnp.concatenate([past_k.astype(k_s.dtype), k_s], axis=1)
    v_all = jnp.concatenate([past_v.astype(v_s.dtype), v_s], axis=1)
    o_s = sb_attend(q_s, k_all, v_all, past_k.shape[1], sb_bias).reshape(b_sz, t_len, SB_HEADS * SB_HD)
    mixed = jnp.concatenate([o_g, o_s], axis=-1).astype(h.dtype)
    return mixed @ w_out, k_s, v_s, s_new


def memory_kv(mem, mem_norm_w, w_mk, w_mv):
    b_sz, m_len, _ = mem.shape
    mn = rmsnorm(mem, mem_norm_w)
    return ((mn @ w_mk).reshape(b_sz, m_len, X_HEADS, X_HD),
            (mn @ w_mv).reshape(b_sz, m_len, X_HEADS, X_HD))


def cross_attend(h, mem_k, mem_v, w_cq, w_co):
    b_sz, t_len, _ = h.shape
    q = (h @ w_cq).reshape(b_sz, t_len, X_HEADS, X_HD).astype(jnp.float32)
    s = jnp.einsum('bthd,bmhd->bhtm', q, mem_k.astype(jnp.float32)) * (X_HD ** -0.5)
    p = jax.nn.softmax(s, axis=-1)
    o = jnp.einsum('bhtm,bmhd->bthd', p, mem_v.astype(jnp.float32)).reshape(b_sz, t_len, X_HEADS * X_HD)
    return o.astype(h.dtype) @ w_co


def layer(x, mem_k, mem_v, past_k, past_v, s0,
          ffn1_norm_w, ffn1_w_gate, ffn1_w_up, ffn1_w_down,
          mix_norm_w, w_in, w_gate2, b_gate2, gla_norm_w, sb_bias, w_out,
          xattn_norm_w, w_cq, w_co,
          ffn2_norm_w, ffn2_w_gate, ffn2_w_up, ffn2_w_down):
    x = x + FFN_RES * swiglu(rmsnorm(x, ffn1_norm_w), ffn1_w_gate, ffn1_w_up, ffn1_w_down)
    mix, k_new, v_new, s_new = token_mix(rmsnorm(x, mix_norm_w), past_k, past_v, s0,
                                         w_in, w_gate2, b_gate2, gla_norm_w, sb_bias, w_out)
    x = x + mix
    x = x + cross_attend(rmsnorm(x, xattn_norm_w), mem_k, mem_v, w_cq, w_co)
    x = x + FFN_RES * swiglu(rmsnorm(x, ffn2_norm_w), ffn2_w_gate, ffn2_w_up, ffn2_w_down)
    return x, k_new, v_new, s_new


def setup_inputs(seed: int = 0) -> dict:
    key = jax.random.key(seed)
    keys = iter(jax.random.split(key, 48))
    f32 = jnp.float32

    def nrm(shape, scale):
        return scale * jax.random.normal(next(keys), shape, f32)

    def gain(shape):
        return 1.0 + 0.02 * jax.random.normal(next(keys), shape, f32)

    n_pages = PAST_LEN // PAGE_SIZE
    n_used = DEC_BATCH * n_pages
    n_pool = n_used + max(1, n_used // 4)
    page_table = jax.random.permutation(next(keys), n_pool)[:n_used].reshape(DEC_BATCH, n_pages).astype(jnp.int32)
    d = D_MODEL
    return {
        'x_prompt': nrm((BATCH, SEQ, d), 1.0),
        'x_sample': nrm((DEC_BATCH, DEC_SEQ, d), 1.0),
        'mem_prompt': nrm((BATCH, N_MEM, d), 1.0),
        'cache_sb_k': nrm((DEPTH, n_pool, PAGE_SIZE, SB_HEADS, SB_HD), 1.0),
        'cache_sb_v': nrm((DEPTH, n_pool, PAGE_SIZE, SB_HEADS, SB_HD), 1.0),
        'page_table': page_table,
        'state_gla': nrm((DEPTH, DEC_BATCH, GLA_HEADS, GLA_DK, GLA_DV), 0.5),
        'cache_mem_k': nrm((DEPTH, DEC_BATCH, N_MEM, X_HEADS, X_HD), 1.0),
        'cache_mem_v': nrm((DEPTH, DEC_BATCH, N_MEM, X_HEADS, X_HD), 1.0),
        'ffn1_norm_w': gain((DEPTH, d)),
        'ffn1_w_gate': nrm((DEPTH, d, D_FF), d ** -0.5),
        'ffn1_w_up': nrm((DEPTH, d, D_FF), d ** -0.5),
        'ffn1_w_down': nrm((DEPTH, D_FF, d), D_FF ** -0.5),
        'mix_norm_w': gain((DEPTH, d)),
        'w_in': nrm((DEPTH, d, IN_WIDTH), d ** -0.5),
        'w_gate2': nrm((DEPTH, GLA_GATE_RANK, GLA_HEADS * GLA_DK), GLA_GATE_RANK ** -0.5),
        'b_gate2': nrm((DEPTH, GLA_HEADS * GLA_DK), 0.1),
        'gla_norm_w': gain((DEPTH, GLA_HEADS, GLA_DV)),
        'sb_bias': SB_BIAS_INIT + nrm((DEPTH, SB_HEADS), 0.1),
        'w_out': nrm((DEPTH, MIX_WIDTH, d), MIX_WIDTH ** -0.5),
        'xattn_norm_w': gain((DEPTH, d)),
        'mem_norm_w': gain((DEPTH, d)),
        'w_mk': nrm((DEPTH, d, X_HEADS * X_HD), d ** -0.5),
        'w_mv': nrm((DEPTH, d, X_HEADS * X_HD), d ** -0.5),
        'w_cq': nrm((DEPTH, d, X_HEADS * X_HD), d ** -0.5),
        'w_co': nrm((DEPTH, X_HEADS * X_HD, d), (X_HEADS * X_HD) ** -0.5),
        'ffn2_norm_w': gain((DEPTH, d)),
        'ffn2_w_gate': nrm((DEPTH, d, D_FF), d ** -0.5),
        'ffn2_w_up': nrm((DEPTH, d, D_FF), d ** -0.5),
        'ffn2_w_down': nrm((DEPTH, D_FF, d), D_FF ** -0.5),
        'final_norm_w': gain((d,)),
    }


def reference(x_prompt, x_sample, mem_prompt, cache_sb_k, cache_sb_v, page_table, state_gla,
              cache_mem_k, cache_mem_v,
              ffn1_norm_w, ffn1_w_gate, ffn1_w_up, ffn1_w_down,
              mix_norm_w, w_in, w_gate2, b_gate2, gla_norm_w, sb_bias, w_out,
              xattn_norm_w, mem_norm_w, w_mk, w_mv, w_cq, w_co,
              ffn2_norm_w, ffn2_w_gate, ffn2_w_up, ffn2_w_down, final_norm_w):
    b_p = x_prompt.shape[0]
    b_d, n_pages = page_table.shape
    past_len = n_pages * cache_sb_k.shape[2]
    xp = x_prompt
    xs = x_sample
    sbk_p, sbv_p, gla_p, memk_p, memv_p = [], [], [], [], []
    sbk_s, sbv_s, gla_s = [], [], []
    for l in range(DEPTH):
        lw = (ffn1_norm_w[l], ffn1_w_gate[l], ffn1_w_up[l], ffn1_w_down[l],
              mix_norm_w[l], w_in[l], w_gate2[l], b_gate2[l], gla_norm_w[l], sb_bias[l], w_out[l],
              xattn_norm_w[l], w_cq[l], w_co[l],
              ffn2_norm_w[l], ffn2_w_gate[l], ffn2_w_up[l], ffn2_w_down[l])
        mk_p, mv_p = memory_kv(mem_prompt, mem_norm_w[l], w_mk[l], w_mv[l])
        empty = jnp.zeros((b_p, 0, SB_HEADS, SB_HD), xp.dtype)
        s0_p = jnp.zeros((b_p, GLA_HEADS, GLA_DK, GLA_DV), jnp.float32)
        xp, k_p, v_p, s_p = layer(xp, mk_p, mv_p, empty, empty, s0_p, *lw)
        past_k = cache_sb_k[l][page_table].reshape(b_d, past_len, SB_HEADS, SB_HD)
        past_v = cache_sb_v[l][page_table].reshape(b_d, past_len, SB_HEADS, SB_HD)
        xs, k_s, v_s, s_s = layer(xs, cache_mem_k[l], cache_mem_v[l], past_k, past_v, state_gla[l], *lw)
        sbk_p.append(k_p)
        sbv_p.append(v_p)
        gla_p.append(s_p)
        memk_p.append(mk_p)
        memv_p.append(mv_p)
        sbk_s.append(k_s)
        sbv_s.append(v_s)
        gla_s.append(s_s)
    y_prompt = rmsnorm(xp, final_norm_w)
    y_sample = rmsnorm(xs, final_norm_w)
    return (y_prompt, y_sample,
            jnp.stack(sbk_p), jnp.stack(sbv_p), jnp.stack(gla_p), jnp.stack(memk_p), jnp.stack(memv_p),
            jnp.stack(sbk_s), jnp.stack(sbv_s), jnp.stack(gla_s))
```

```python
import functools

import jax
import jax.numpy as jnp
from jax import lax
from jax.experimental import pallas as pl
from jax.experimental.pallas import tpu as pltpu

F32 = jnp.float32
BF16 = jnp.bfloat16

NORM_EPS = 1e-6
FFN_RES = 0.5
GLA_HEADS = 4
GLA_DK = 64
GLA_DV = 128
GLA_GATE_RANK = 16
GLA_GATE_TAU = 16.0
GLA_CHUNK = 64
GLA_TILE = 128
SB_HEADS = 8
SB_HD = 64
X_HEADS = 4
LANE = 128
MXU_DIM = 256
VMEM_LIMIT = 56 * 1024 * 1024


def _cparams(sem):
    return pltpu.CompilerParams(dimension_semantics=sem, vmem_limit_bytes=VMEM_LIMIT)


def _rms(x, w):
    return x * lax.rsqrt(jnp.mean(x * x, axis=-1, keepdims=True) + NORM_EPS) * w


def _softplus(z):
    return jnp.maximum(z, 0.0) + jnp.log(1.0 + jnp.exp(-jnp.abs(z)))


def _log_sigmoid(z):
    return -_softplus(-z)


def _sigmoid(z):
    return 1.0 / (1.0 + jnp.exp(-z))


def _dot(a, b):
    return jnp.dot(a, b, preferred_element_type=F32)


def _dot_nt(a, b):
    return lax.dot_general(a, b, (((1,), (1,)), ((), ())), preferred_element_type=F32)


def _split_bf16(x):
    hi = x.astype(BF16)
    lo = (x - hi.astype(F32)).astype(BF16)
    return hi, lo


def _memkv_kernel(mem_ref, nw_ref, wk_ref, wv_ref, k_ref, v_ref):
    mn = _rms(mem_ref[0], nw_ref[...]).astype(BF16)
    k_ref[0] = _dot(mn, wk_ref[...])
    v_ref[0] = _dot(mn, wv_ref[...])


def _memory_kv(mem, nw, wk, wv):
    b, m, d = mem.shape
    full = lambda shape: pl.BlockSpec(shape, lambda i: (0,) * len(shape))
    row = pl.BlockSpec((1, m, d), lambda i: (i, 0, 0))
    return pl.pallas_call(
        _memkv_kernel,
        grid=(b,),
        in_specs=[row, full((1, d)), full((d, d)), full((d, d))],
        out_specs=[row, row],
        out_shape=[jax.ShapeDtypeStruct((b, m, d), F32)] * 2,
        compiler_params=_cparams(("arbitrary",)),
        name="memory_kv",
    )(mem, nw.reshape(1, d), wk.astype(BF16), wv.astype(BF16))


def _ffn_kernel(x_ref, nw_ref, wg_ref, wu_ref, wd_ref, fw_ref, o_ref, *, final_norm):
    x = x_ref[...]
    h = _rms(x, nw_ref[...]).astype(BF16)
    acc = jnp.zeros_like(x)
    for c in range(wg_ref.shape[1] // MXU_DIM):
        sl = slice(c * MXU_DIM, (c + 1) * MXU_DIM)
        g = _dot(h, wg_ref[:, sl])
        u = _dot(h, wu_ref[:, sl])
        a = (g * _sigmoid(g) * u).astype(BF16)
        acc = acc + _dot(a, wd_ref[sl, :])
    y = x + FFN_RES * acc
    if final_norm:
        y = _rms(y, fw_ref[...])
    o_ref[...] = y


def _pad_ff(w, axis):
    ff = w.shape[axis]
    pad = (-ff) % MXU_DIM
    cfg = [(0, 0), (0, 0)]
    cfg[axis] = (0, pad)
    return jnp.pad(w, cfg).astype(BF16)


def _ffn(x, nw, wg, wu, wd, fw, final_norm):
    rows, d = x.shape
    tm = min(rows, 512)
    assert rows % tm == 0
    ffp = wg.shape[1]
    const = lambda shape: pl.BlockSpec(shape, lambda i: (0, 0), pipeline_mode=pl.Buffered(1))
    row = pl.BlockSpec((tm, d), lambda i: (i, 0))
    return pl.pallas_call(
        functools.partial(_ffn_kernel, final_norm=final_norm),
        grid=(rows // tm,),
        in_specs=[row, const((1, d)), const((d, ffp)), const((d, ffp)), const((ffp, d)), const((1, d))],
        out_specs=row,
        out_shape=jax.ShapeDtypeStruct((rows, d), F32),
        compiler_params=_cparams(("arbitrary",)),
        name="ffn_final" if final_norm else "ffn",
    )(x, nw.reshape(1, d), wg, wu, wd, fw.reshape(1, d))


def _inproj_kernel(x_ref, nw_ref, wn_ref, wt_ref, wglrt_ref, wg2_ref, bg2_ref, wg2t_ref, bg2t_ref,
                   qg_ref, vg_ref, rg_ref, la_ref, qs_ref, ks_ref, vs_ref,
                   qgt_ref, kgt_ref, lat_ref, kst_ref):
    nqk = GLA_HEADS * GLA_DK
    nv = GLA_HEADS * GLA_DV
    ns = SB_HEADS * SB_HD
    h = _rms(x_ref[...], nw_ref[...]).astype(BF16)
    y = _dot(h, wn_ref[...])
    o = 0
    qg_ref[...] = y[:, o:o + nqk] * (GLA_DK ** -0.5)
    o += nqk
    vg_ref[...] = y[:, o:o + nv]
    o += nv
    rg_ref[...] = y[:, o:o + nv]
    o += nv
    qs_ref[...] = (y[:, o:o + ns] * (SB_HD ** -0.5)).astype(BF16)
    o += ns
    ks_ref[...] = y[:, o:o + ns]
    o += ns
    vs_ref[...] = y[:, o:o + ns]
    o += ns
    glr = y[:, o:o + LANE].astype(BF16)
    la_ref[...] = _log_sigmoid(_dot(glr, wg2_ref[...]) + bg2_ref[...]) * (1.0 / GLA_GATE_TAU)
    yt = _dot_nt(wt_ref[...], h)
    qgt_ref[...] = yt[0:nqk] * (GLA_DK ** -0.5)
    kgt_ref[...] = yt[nqk:2 * nqk]
    kst_ref[...] = yt[2 * nqk:2 * nqk + ns].astype(BF16)
    glrt = _dot_nt(wglrt_ref[...], h).astype(BF16)
    lat_ref[...] = _log_sigmoid(_dot(wg2t_ref[...], glrt) + bg2t_ref[...]) * (1.0 / GLA_GATE_TAU)


def _prep_inproj_weights(w_in, w_gate2, b_gate2):
    nqk = GLA_HEADS * GLA_DK
    nv = GLA_HEADS * GLA_DV
    ns = SB_HEADS * SB_HD
    sizes = (nqk, nqk, nv, GLA_GATE_RANK, nv, ns, ns, ns)
    offs = [0]
    for s in sizes:
        offs.append(offs[-1] + s)
    qg, kg, vg, glr, rg, qs, ks, vs = [w_in[:, offs[i]:offs[i + 1]] for i in range(8)]
    glr_pad = jnp.pad(glr, ((0, 0), (0, LANE - GLA_GATE_RANK)))
    wn = jnp.concatenate([qg, vg, rg, qs, ks, vs, glr_pad], axis=1).astype(BF16)
    wt = jnp.concatenate([qg, kg, ks], axis=1).T.astype(BF16)
    wglrt = glr.T.astype(BF16)
    wg2 = jnp.pad(w_gate2, ((0, LANE - GLA_GATE_RANK), (0, 0))).astype(BF16)
    wg2t = w_gate2.T.astype(BF16)
    return wn, wt, wglrt, wg2, b_gate2.reshape(1, nqk), wg2t, b_gate2.reshape(nqk, 1)


def _inproj(x, nw, weights):
    rows, d = x.shape
    wn, wt, wglrt, wg2, bg2, wg2t, bg2t = weights
    tm = min(rows, 256)
    assert rows % tm == 0
    nqk = GLA_HEADS * GLA_DK
    nv = GLA_HEADS * GLA_DV
    ns = SB_HEADS * SB_HD
    const = lambda a: pl.BlockSpec(a.shape, lambda i: (0, 0), pipeline_mode=pl.Buffered(1))
    row = lambda n: pl.BlockSpec((tm, n), lambda i: (i, 0))
    col = lambda n: pl.BlockSpec((n, tm), lambda i: (0, i))
    sds = jax.ShapeDtypeStruct
    nw2 = nw.reshape(1, d)
    return pl.pallas_call(
        _inproj_kernel,
        grid=(rows // tm,),
        in_specs=[row(d), const(nw2), const(wn), const(wt), const(wglrt), const(wg2), const(bg2),
                  const(wg2t), const(bg2t)],
        out_specs=[row(nqk), row(nv), row(nv), row(nqk), row(ns), row(ns), row(ns),
                   col(nqk), col(nqk), col(nqk), col(ns)],
        out_shape=[sds((rows, nqk), F32), sds((rows, nv), F32), sds((rows, nv), F32), sds((rows, nqk), F32),
                   sds((rows, ns), BF16), sds((rows, ns), F32), sds((rows, ns), F32),
                   sds((nqk, rows), F32), sds((nqk, rows), F32), sds((nqk, rows), F32), sds((ns, rows), BF16)],
        compiler_params=_cparams(("arbitrary",)),
        name="in_proj",
    )(x, nw2, wn, wt, wglrt, wg2, bg2, wg2t, bg2t)


def _gla_out(o, gnw, r):
    o = o * lax.rsqrt(jnp.mean(o * o, axis=-1, keepdims=True) + NORM_EPS) * gnw
    return o * (r * _sigmoid(r))


def _gla_prompt_kernel(q_ref, v_ref, r_ref, la_ref, kt_ref, lat_ref, gnw_ref, o_ref, s_out_ref, s_ref):
    i = pl.program_id(0)
    c = GLA_CHUNK

    @pl.when(i == 0)
    def _():
        s_ref[...] = jnp.zeros_like(s_ref)

    rr = lax.broadcasted_iota(jnp.int32, (c, c), 0)
    cc = lax.broadcasted_iota(jnp.int32, (c, c), 1)
    causal = cc <= rr
    tri = jnp.where(causal, 1.0, 0.0).astype(BF16)
    trit = jnp.where(rr <= cc, 1.0, 0.0).astype(BF16)
    mid = c // 2

    for ch in range(GLA_TILE // c):
        rows = slice(ch * c, (ch + 1) * c)
        la_hi, la_lo = _split_bf16(la_ref[rows, :])
        b = _dot(tri, la_hi) + _dot(tri, la_lo)
        lat_hi, lat_lo = _split_bf16(lat_ref[:, rows])
        bt = _dot(lat_hi, trit) + _dot(lat_lo, trit)
        q = q_ref[rows, :]
        kt = kt_ref[:, rows]
        qe = q * jnp.exp(b)
        qm = q * jnp.exp(b - b[mid:mid + 1, :])
        ktm = kt * jnp.exp(bt[:, mid:mid + 1] - bt)
        ktd = kt * jnp.exp(bt[:, c - 1:c] - bt)
        dec = jnp.exp(bt[:, c - 1:c])
        for h in range(GLA_HEADS):
            ks = slice(h * GLA_DK, (h + 1) * GLA_DK)
            vs = slice(h * GLA_DV, (h + 1) * GLA_DV)
            s = s_ref[h]
            v = v_ref[rows, vs]
            att = jnp.where(causal, _dot(qm[:, ks], ktm[ks, :]), 0.0)
            o = _dot(qe[:, ks], s) + _dot(att, v)
            s_ref[h] = dec[ks, :] * s + _dot(ktd[ks, :], v)
            o_ref[rows, vs] = _gla_out(o, gnw_ref[h:h + 1, :], r_ref[rows, vs])

    @pl.when(i == pl.num_programs(0) - 1)
    def _():
        s_out_ref[...] = s_ref[...]


def _gla_prompt(qg, vg, rg, la, kgt, lat, gnw):
    t = qg.shape[0]
    nqk = GLA_HEADS * GLA_DK
    nv = GLA_HEADS * GLA_DV
    tg = GLA_TILE
    assert t % tg == 0
    row = lambda n: pl.BlockSpec((tg, n), lambda i: (i, 0))
    col = lambda n: pl.BlockSpec((n, tg), lambda i: (0, i))
    return pl.pallas_call(
        _gla_prompt_kernel,
        grid=(t // tg,),
        in_specs=[row(nqk), row(nv), row(nv), row(nqk), col(nqk), col(nqk),
                  pl.BlockSpec((GLA_HEADS, GLA_DV), lambda i: (0, 0))],
        out_specs=[row(nv), pl.BlockSpec((GLA_HEADS, GLA_DK, GLA_DV), lambda i: (0, 0, 0))],
        out_shape=[jax.ShapeDtypeStruct((t, nv), F32),
                   jax.ShapeDtypeStruct((GLA_HEADS, GLA_DK, GLA_DV), F32)],
        scratch_shapes=[pltpu.VMEM((GLA_HEADS, GLA_DK, GLA_DV), F32)],
        compiler_params=_cparams(("arbitrary",)),
        name="gla_prompt",
    )(qg, vg, rg, la, kgt, lat, gnw)


def _gla_sample_kernel(qt_ref, kt_ref, lat_ref, v_ref, r_ref, s0_ref, gnw_ref, o_ref, s_out_ref):
    n_t = v_ref.shape[1]
    for h in range(GLA_HEADS):
        ks = slice(h * GLA_DK, (h + 1) * GLA_DK)
        vs = slice(h * GLA_DV, (h + 1) * GLA_DV)
        s = s0_ref[0, h]
        for t in range(n_t):
            a = jnp.exp(lat_ref[0, ks, t:t + 1])
            s = a * s + kt_ref[0, ks, t:t + 1] * v_ref[0, t:t + 1, vs]
            o = jnp.sum(qt_ref[0, ks, t:t + 1] * s, axis=0, keepdims=True)
            o_ref[0, t:t + 1, vs] = _gla_out(o, gnw_ref[h:h + 1, :], r_ref[0, t:t + 1, vs])
        s_out_ref[0, h] = s


def _gla_sample(qgt, kgt, lat, vg, rg, s0, gnw):
    b, n_t, nv = vg.shape
    nqk = GLA_HEADS * GLA_DK
    colb = pl.BlockSpec((1, nqk, n_t), lambda i: (i, 0, 0))
    rowb = pl.BlockSpec((1, n_t, nv), lambda i: (i, 0, 0))
    st = pl.BlockSpec((1, GLA_HEADS, GLA_DK, GLA_DV), lambda i: (i, 0, 0, 0))
    return pl.pallas_call(
        _gla_sample_kernel,
        grid=(b,),
        in_specs=[colb, colb, colb, rowb, rowb, st, pl.BlockSpec((GLA_HEADS, GLA_DV), lambda i: (0, 0))],
        out_specs=[rowb, st],
        out_shape=[jax.ShapeDtypeStruct((b, n_t, nv), F32),
                   jax.ShapeDtypeStruct((b, GLA_HEADS, GLA_DK, GLA_DV), F32)],
        compiler_params=_cparams(("arbitrary",)),
        name="gla_sample",
    )(qgt, kgt, lat, vg, rg, s0, gnw)


def _neg_tri(n, dtype):
    rr = lax.broadcasted_iota(jnp.int32, (n, n), 0)
    cc = lax.broadcasted_iota(jnp.int32, (n, n), 1)
    return jnp.where(rr >= cc, -1.0, 0.0).astype(dtype)


def _sb_prompt_kernel(bias_ref, q_ref, kt_ref, v_ref, o_ref, *, blk):
    h = pl.program_id(0)
    i = pl.program_id(1)
    q = q_ref[0]
    bias = bias_ref[h]
    ntri = _neg_tri(blk, BF16)

    def block(j, carry, acc, diagonal):
        z = _dot(q, kt_ref[0, j]) + bias
        sp = _softplus(z)
        if diagonal:
            rr = lax.broadcasted_iota(jnp.int32, (blk, blk), 0)
            cc = lax.broadcasted_iota(jnp.int32, (blk, blk), 1)
            mask = cc < rr
            sp = jnp.where(mask, sp, 0.0)
        lsum = _dot(sp.astype(BF16), ntri)
        w = jnp.exp(z + lsum + carry)
        if diagonal:
            w = jnp.where(mask, w, 0.0)
        acc = acc + _dot(w.astype(BF16), v_ref[0, j])
        carry = carry + lsum[:, 0:1]
        return carry, acc

    carry0 = jnp.zeros((blk, 1), F32)
    acc0 = jnp.zeros((blk, SB_HD), F32)
    carry, acc = block(i, carry0, acc0, True)

    def body(n, ca):
        return block(i - 1 - n, ca[0], ca[1], False)

    carry, acc = lax.fori_loop(0, i, body, (carry, acc))
    o_ref[0] = acc


def _sb_prompt(qs, kst, vs, bias):
    t = qs.shape[0]
    blk = min(t, MXU_DIM)
    assert t % blk == 0
    nb = t // blk
    q3 = qs.reshape(t, SB_HEADS, SB_HD).transpose(1, 0, 2)
    kt4 = kst.reshape(SB_HEADS, SB_HD, nb, blk).transpose(0, 2, 1, 3)
    v4 = vs.astype(BF16).reshape(nb, blk, SB_HEADS, SB_HD).transpose(2, 0, 1, 3)
    o3 = pl.pallas_call(
        functools.partial(_sb_prompt_kernel, blk=blk),
        grid=(SB_HEADS, nb),
        in_specs=[pl.BlockSpec(memory_space=pltpu.SMEM),
                  pl.BlockSpec((1, blk, SB_HD), lambda h, i: (h, i, 0)),
                  pl.BlockSpec((1, nb, SB_HD, blk), lambda h, i: (h, 0, 0, 0)),
                  pl.BlockSpec((1, nb, blk, SB_HD), lambda h, i: (h, 0, 0, 0))],
        out_specs=pl.BlockSpec((1, blk, SB_HD), lambda h, i: (h, i, 0)),
        out_shape=jax.ShapeDtypeStruct((SB_HEADS, t, SB_HD), F32),
        compiler_params=_cparams(("arbitrary", "arbitrary")),
        name="sb_prompt",
    )(bias, q3, kt4, v4)
    return o3.transpose(1, 0, 2).reshape(t, SB_HEADS * SB_HD)


def _sb_sample_kernel(pt_ref, q_ref, bias_ref, kn_ref, vn_ref, kp_ref, vp_ref, o_ref, carry_ref, acc_ref, *, n_t):
    p = pl.program_id(1)
    q = q_ref[0]
    bias = bias_ref[...]
    page = kp_ref.shape[1]

    @pl.when(p == 0)
    def _():
        t_row = lax.broadcasted_iota(jnp.int32, (q.shape[0], 1), 0) % n_t
        carry = jnp.zeros((q.shape[0], 1), F32)
        acc = jnp.zeros(q.shape, F32)
        for s in range(n_t - 2, -1, -1):
            z = jnp.sum(q * kn_ref[0, s:s + 1, :], axis=-1, keepdims=True) + bias
            m = s < t_row
            carry = carry - jnp.where(m, _softplus(z), 0.0)
            w = jnp.where(m, jnp.exp(z + carry), 0.0)
            acc = acc + w * vn_ref[0, s:s + 1, :]
        carry_ref[...] = carry
        acc_ref[...] = acc

    z = _dot_nt(q, kp_ref[0]) + bias
    sp = _softplus(z)
    lsum = _dot(sp, _neg_tri(page, F32))
    w = jnp.exp(z + lsum + carry_ref[...])
    acc_ref[...] += _dot(w, vp_ref[0])
    carry_ref[...] += lsum[:, 0:1]

    @pl.when(p == pl.num_programs(1) - 1)
    def _():
        o_ref[0] = acc_ref[...]


def _sb_sample(qs, ks_new, vs_new, pool_k, pool_v, page_table, bias):
    b, n_t, hd = qs.shape
    n_pages = page_table.shape[1]
    page = pool_k.shape[1]
    nr = SB_HEADS * n_t
    head_of_lane = jnp.arange(hd, dtype=jnp.int32) // SB_HD
    head_of_row = jnp.arange(nr, dtype=jnp.int32) // n_t
    sel = (head_of_row[:, None] == head_of_lane[None, :])
    q_rows = jnp.tile(qs.astype(F32), (1, SB_HEADS, 1)).reshape(b, SB_HEADS, n_t, hd).reshape(b, nr, hd)
    q_bd = jnp.where(sel[None], q_rows, 0.0)
    bias_col = bias[head_of_row].reshape(nr, 1)
    grid_spec = pltpu.PrefetchScalarGridSpec(
        num_scalar_prefetch=1,
        grid=(b, n_pages),
        in_specs=[pl.BlockSpec((1, nr, hd), lambda i, p, pt: (i, 0, 0)),
                  pl.BlockSpec((nr, 1), lambda i, p, pt: (0, 0)),
                  pl.BlockSpec((1, n_t, hd), lambda i, p, pt: (i, 0, 0)),
                  pl.BlockSpec((1, n_t, hd), lambda i, p, pt: (i, 0, 0)),
                  pl.BlockSpec((1, page, hd), lambda i, p, pt: (pt[i, n_pages - 1 - p], 0, 0)),
                  pl.BlockSpec((1, page, hd), lambda i, p, pt: (pt[i, n_pages - 1 - p], 0, 0))],
        out_specs=pl.BlockSpec((1, nr, hd), lambda i, p, pt: (i, 0, 0)),
        scratch_shapes=[pltpu.VMEM((nr, 1), F32), pltpu.VMEM((nr, hd), F32)],
    )
    o_all = pl.pallas_call(
        functools.partial(_sb_sample_kernel, n_t=n_t),
        grid_spec=grid_spec,
        out_shape=jax.ShapeDtypeStruct((b, nr, hd), F32),
        compiler_params=_cparams(("arbitrary", "arbitrary")),
        name="sb_sample",
    )(page_table, q_bd, bias_col, ks_new, vs_new, pool_k, pool_v)
    o5 = o_all.reshape(b, SB_HEADS, n_t, SB_HEADS, SB_HD)
    idx = jnp.arange(SB_HEADS)
    o_diag = o5[:, idx, :, idx, :]
    return o_diag.transpose(1, 2, 0, 3).reshape(b, n_t, hd)


def _mix_xattn_kernel(x_ref, og_ref, os_ref, wo_ref, nw_ref, wcq_ref, mk_ref, mv_ref, wco_ref, o_ref):
    ng = og_ref.shape[2]
    x = x_ref[0]
    x = x + _dot(og_ref[0].astype(BF16), wo_ref[0:ng, :]) + _dot(os_ref[0].astype(BF16), wo_ref[ng:, :])
    h = _rms(x, nw_ref[...]).astype(BF16)
    d = x.shape[1]
    hd = d // X_HEADS
    q = _dot(h, wcq_ref[...]) * (hd ** -0.5)
    outs = []
    for hh in range(X_HEADS):
        sl = slice(hh * hd, (hh + 1) * hd)
        s = _dot_nt(q[:, sl], mk_ref[0, :, sl])
        s = s - jnp.max(s, axis=-1, keepdims=True)
        e = jnp.exp(s)
        p = e / jnp.sum(e, axis=-1, keepdims=True)
        outs.append(_dot(p, mv_ref[0, :, sl]))
    o = jnp.concatenate(outs, axis=-1).astype(BF16)
    o_ref[0] = x + _dot(o, wco_ref[...])


def _mix_xattn(x, og, osb, wo, nw, wcq, mk, mv, wco):
    b, r, d = x.shape
    m = mk.shape[1]
    tm = min(r, 512)
    assert r % tm == 0
    ng = og.shape[2]
    ns = osb.shape[2]
    const = lambda shape: pl.BlockSpec(shape, lambda i, j: (0, 0), pipeline_mode=pl.Buffered(1))
    row = lambda n: pl.BlockSpec((1, tm, n), lambda i, j: (i, j, 0))
    mem = pl.BlockSpec((1, m, d), lambda i, j: (i, 0, 0))
    return pl.pallas_call(
        _mix_xattn_kernel,
        grid=(b, r // tm),
        in_specs=[row(d), row(ng), row(ns), const((ng + ns, d)), const((1, d)), const((d, d)), mem, mem,
                  const((d, d))],
        out_specs=row(d),
        out_shape=jax.ShapeDtypeStruct((b, r, d), F32),
        compiler_params=_cparams(("arbitrary", "arbitrary")),
        name="mix_xattn",
    )(x, og, osb, wo, nw.reshape(1, d), wcq, mk, mv, wco)


def kernel(x_prompt, x_sample, mem_prompt, cache_sb_k, cache_sb_v, page_table, state_gla, cache_mem_k, cache_mem_v, ffn1_norm_w, ffn1_w_gate, ffn1_w_up, ffn1_w_down, mix_norm_w, w_in, w_gate2, b_gate2, gla_norm_w, sb_bias, w_out, xattn_norm_w, mem_norm_w, w_mk, w_mv, w_cq, w_co, ffn2_norm_w, ffn2_w_gate, ffn2_w_up, ffn2_w_down, final_norm_w):
    depth = w_in.shape[0]
    b_p, t_p, d = x_prompt.shape
    b_d, t_d, _ = x_sample.shape
    assert b_p == 1
    n_pool, page = cache_sb_k.shape[1], cache_sb_k.shape[2]
    hd_s = SB_HEADS * SB_HD
    nv = GLA_HEADS * GLA_DV

    xp = x_prompt.reshape(b_p * t_p, d)
    xs = x_sample.reshape(b_d * t_d, d)
    outs = [[] for _ in range(8)]
    for l in range(depth):
        last = l == depth - 1
        f1 = (_pad_ff(ffn1_w_gate[l], 1), _pad_ff(ffn1_w_up[l], 1), _pad_ff(ffn1_w_down[l], 0))
        f2 = (_pad_ff(ffn2_w_gate[l], 1), _pad_ff(ffn2_w_up[l], 1), _pad_ff(ffn2_w_down[l], 0))
        wproj = _prep_inproj_weights(w_in[l], w_gate2[l], b_gate2[l])
        wo = w_out[l].astype(BF16)
        wcq = w_cq[l].astype(BF16)
        wco = w_co[l].astype(BF16)

        mk_p, mv_p = _memory_kv(mem_prompt, mem_norm_w[l], w_mk[l], w_mv[l])

        x1 = _ffn(xp, ffn1_norm_w[l], *f1, final_norm_w, False)
        qg, vg, rg, la, qs, ks, vs, _, kgt, lat, kst = _inproj(x1, mix_norm_w[l], wproj)
        og, s_p = _gla_prompt(qg, vg, rg, la, kgt, lat, gla_norm_w[l])
        osb = _sb_prompt(qs, kst, vs, sb_bias[l])
        x3 = _mix_xattn(x1.reshape(b_p, t_p, d), og.reshape(b_p, t_p, nv), osb.reshape(b_p, t_p, hd_s),
                        wo, xattn_norm_w[l], wcq, mk_p, mv_p, wco)
        xp = _ffn(x3.reshape(b_p * t_p, d), ffn2_norm_w[l], *f2, final_norm_w, last)
        outs[0].append(ks.reshape(b_p, t_p, SB_HEADS, SB_HD))
        outs[1].append(vs.reshape(b_p, t_p, SB_HEADS, SB_HD))
        outs[2].append(s_p.reshape(b_p, GLA_HEADS, GLA_DK, GLA_DV))
        outs[3].append(mk_p.reshape(b_p, -1, X_HEADS, d // X_HEADS))
        outs[4].append(mv_p.reshape(b_p, -1, X_HEADS, d // X_HEADS))

        x1 = _ffn(xs, ffn1_norm_w[l], *f1, final_norm_w, False)
        _, vg, rg, _, qs, ks, vs, qgt, kgt, lat, _ = _inproj(x1, mix_norm_w[l], wproj)
        tok_major = lambda a: a.reshape(a.shape[0], b_d, t_d).transpose(1, 0, 2)
        og, s_s = _gla_sample(tok_major(qgt), tok_major(kgt), tok_major(lat),
                              vg.reshape(b_d, t_d, nv), rg.reshape(b_d, t_d, nv), state_gla[l], gla_norm_w[l])
        osb = _sb_sample(qs.reshape(b_d, t_d, hd_s), ks.reshape(b_d, t_d, hd_s), vs.reshape(b_d, t_d, hd_s),
                         cache_sb_k[l].reshape(n_pool, page, hd_s), cache_sb_v[l].reshape(n_pool, page, hd_s),
                         page_table, sb_bias[l])
        m_len = cache_mem_k.shape[2]
        x3 = _mix_xattn(x1.reshape(b_d, t_d, d), og, osb, wo, xattn_norm_w[l], wcq,
                        cache_mem_k[l].reshape(b_d, m_len, d), cache_mem_v[l].reshape(b_d, m_len, d), wco)
        xs = _ffn(x3.reshape(b_d * t_d, d), ffn2_norm_w[l], *f2, final_norm_w, last)
        outs[5].append(ks.reshape(b_d, t_d, SB_HEADS, SB_HD))
        outs[6].append(vs.reshape(b_d, t_d, SB_HEADS, SB_HD))
        outs[7].append(s_s)

    stk = [jnp.stack(o) for o in outs]
    return (xp.reshape(b_p, t_p, d), xs.reshape(b_d, t_d, d),
            stk[0], stk[1], stk[2], stk[3], stk[4], stk[5], stk[6], stk[7])
```

```python
import functools

import jax
import jax.numpy as jnp
from jax import lax
from jax.experimental import pallas as pl
from jax.experimental.pallas import tpu as pltpu

F32 = jnp.float32
BF16 = jnp.bfloat16

NORM_EPS = 1e-6
FFN_RES = 0.5
GLA_HEADS = 4
GLA_DK = 64
GLA_DV = 128
GLA_GATE_RANK = 16
GLA_GATE_TAU = 16.0
GLA_CHUNK = 64
GLA_TILE = 128
SB_HEADS = 8
SB_HD = 64
SB_MASKED_LOGIT = -1e30
SB_HEADS_PER_STEP = 2
SB_PAGES_PER_STEP = 8
LOG2E = 1.4426950408889634
X_HEADS = 4
LANE = 128
MXU_DIM = 256
VMEM_LIMIT = 56 * 1024 * 1024


def _cparams(sem):
    return pltpu.CompilerParams(dimension_semantics=sem, vmem_limit_bytes=VMEM_LIMIT)


def _rms(x, w):
    return x * lax.rsqrt(jnp.mean(x * x, axis=-1, keepdims=True) + NORM_EPS) * w


def _softplus(z):
    return jnp.maximum(z, 0.0) + jnp.log(1.0 + jnp.exp(-jnp.abs(z)))


def _log_sigmoid(z):
    return -_softplus(-z)


def _sigmoid(z):
    return 1.0 / (1.0 + jnp.exp(-z))


def _dot(a, b):
    return jnp.dot(a, b, preferred_element_type=F32)


def _dot_nt(a, b):
    return lax.dot_general(a, b, (((1,), (1,)), ((), ())), preferred_element_type=F32)


def _split_bf16(x):
    hi = x.astype(BF16)
    lo = (x - hi.astype(F32)).astype(BF16)
    return hi, lo


def _memkv_kernel(mem_ref, nw_ref, wk_ref, wv_ref, k_ref, v_ref):
    mn = _rms(mem_ref[0], nw_ref[...]).astype(BF16)
    k_ref[0] = _dot(mn, wk_ref[...])
    v_ref[0] = _dot(mn, wv_ref[...])


def _memory_kv(mem, nw, wk, wv):
    b, m, d = mem.shape
    full = lambda shape: pl.BlockSpec(shape, lambda i: (0,) * len(shape))
    row = pl.BlockSpec((1, m, d), lambda i: (i, 0, 0))
    return pl.pallas_call(
        _memkv_kernel,
        grid=(b,),
        in_specs=[row, full((1, d)), full((d, d)), full((d, d))],
        out_specs=[row, row],
        out_shape=[jax.ShapeDtypeStruct((b, m, d), F32)] * 2,
        compiler_params=_cparams(("arbitrary",)),
        name="memory_kv",
    )(mem, nw.reshape(1, d), wk.astype(BF16), wv.astype(BF16))


def _ffn_kernel(x_ref, nw_ref, wg_ref, wu_ref, wd_ref, fw_ref, o_ref, *, final_norm):
    x = x_ref[...]
    h = _rms(x, nw_ref[...]).astype(BF16)
    acc = jnp.zeros_like(x)
    for c in range(wg_ref.shape[1] // MXU_DIM):
        sl = slice(c * MXU_DIM, (c + 1) * MXU_DIM)
        g = _dot(h, wg_ref[:, sl])
        u = _dot(h, wu_ref[:, sl])
        a = (g * _sigmoid(g) * u).astype(BF16)
        acc = acc + _dot(a, wd_ref[sl, :])
    y = x + FFN_RES * acc
    if final_norm:
        y = _rms(y, fw_ref[...])
    o_ref[...] = y


def _pad_ff(w, axis):
    ff = w.shape[axis]
    pad = (-ff) % MXU_DIM
    cfg = [(0, 0), (0, 0)]
    cfg[axis] = (0, pad)
    return jnp.pad(w, cfg).astype(BF16)


def _ffn(x, nw, wg, wu, wd, fw, final_norm):
    rows, d = x.shape
    tm = min(rows, 512)
    assert rows % tm == 0
    ffp = wg.shape[1]
    const = lambda shape: pl.BlockSpec(shape, lambda i: (0, 0), pipeline_mode=pl.Buffered(1))
    row = pl.BlockSpec((tm, d), lambda i: (i, 0))
    return pl.pallas_call(
        functools.partial(_ffn_kernel, final_norm=final_norm),
        grid=(rows // tm,),
        in_specs=[row, const((1, d)), const((d, ffp)), const((d, ffp)), const((ffp, d)), const((1, d))],
        out_specs=row,
        out_shape=jax.ShapeDtypeStruct((rows, d), F32),
        compiler_params=_cparams(("arbitrary",)),
        name="ffn_final" if final_norm else "ffn",
    )(x, nw.reshape(1, d), wg, wu, wd, fw.reshape(1, d))


def _inproj_kernel(x_ref, nw_ref, wn_ref, wt_ref, wglrt_ref, wg2_ref, bg2_ref, wg2t_ref, bg2t_ref,
                   qg_ref, vg_ref, rg_ref, la_ref, qs_ref, ks_ref, vs_ref,
                   qgt_ref, kgt_ref, lat_ref, kst_ref):
    nqk = GLA_HEADS * GLA_DK
    nv = GLA_HEADS * GLA_DV
    ns = SB_HEADS * SB_HD
    h = _rms(x_ref[...], nw_ref[...]).astype(BF16)
    y = _dot(h, wn_ref[...])
    o = 0
    qg_ref[...] = y[:, o:o + nqk] * (GLA_DK ** -0.5)
    o += nqk
    vg_ref[...] = y[:, o:o + nv]
    o += nv
    rg_ref[...] = y[:, o:o + nv]
    o += nv
    qs_ref[...] = (y[:, o:o + ns] * (SB_HD ** -0.5 * LOG2E)).astype(BF16)
    o += ns
    ks_ref[...] = y[:, o:o + ns]
    o += ns
    vs_ref[...] = y[:, o:o + ns]
    o += ns
    glr = y[:, o:o + LANE].astype(BF16)
    la_ref[...] = _log_sigmoid(_dot(glr, wg2_ref[...]) + bg2_ref[...]) * (1.0 / GLA_GATE_TAU)
    yt = _dot_nt(wt_ref[...], h)
    qgt_ref[...] = yt[0:nqk] * (GLA_DK ** -0.5)
    kgt_ref[...] = yt[nqk:2 * nqk]
    kst_ref[...] = yt[2 * nqk:2 * nqk + ns].astype(BF16)
    glrt = _dot_nt(wglrt_ref[...], h).astype(BF16)
    lat_ref[...] = _log_sigmoid(_dot(wg2t_ref[...], glrt) + bg2t_ref[...]) * (1.0 / GLA_GATE_TAU)


def _prep_inproj_weights(w_in, w_gate2, b_gate2):
    nqk = GLA_HEADS * GLA_DK
    nv = GLA_HEADS * GLA_DV
    ns = SB_HEADS * SB_HD
    sizes = (nqk, nqk, nv, GLA_GATE_RANK, nv, ns, ns, ns)
    offs = [0]
    for s in sizes:
        offs.append(offs[-1] + s)
    qg, kg, vg, glr, rg, qs, ks, vs = [w_in[:, offs[i]:offs[i + 1]] for i in range(8)]
    glr_pad = jnp.pad(glr, ((0, 0), (0, LANE - GLA_GATE_RANK)))
    wn = jnp.concatenate([qg, vg, rg, qs, ks, vs, glr_pad], axis=1).astype(BF16)
    wt = jnp.concatenate([qg, kg, ks], axis=1).T.astype(BF16)
    wglrt = glr.T.astype(BF16)
    wg2 = jnp.pad(w_gate2, ((0, LANE - GLA_GATE_RANK), (0, 0))).astype(BF16)
    wg2t = w_gate2.T.astype(BF16)
    return wn, wt, wglrt, wg2, b_gate2.reshape(1, nqk), wg2t, b_gate2.reshape(nqk, 1)


def _inproj(x, nw, weights):
    rows, d = x.shape
    wn, wt, wglrt, wg2, bg2, wg2t, bg2t = weights
    tm = min(rows, 256)
    assert rows % tm == 0
    nqk = GLA_HEADS * GLA_DK
    nv = GLA_HEADS * GLA_DV
    ns = SB_HEADS * SB_HD
    const = lambda a: pl.BlockSpec(a.shape, lambda i: (0, 0), pipeline_mode=pl.Buffered(1))
    row = lambda n: pl.BlockSpec((tm, n), lambda i: (i, 0))
    col = lambda n: pl.BlockSpec((n, tm), lambda i: (0, i))
    sds = jax.ShapeDtypeStruct
    nw2 = nw.reshape(1, d)
    return pl.pallas_call(
        _inproj_kernel,
        grid=(rows // tm,),
        in_specs=[row(d), const(nw2), const(wn), const(wt), const(wglrt), const(wg2), const(bg2),
                  const(wg2t), const(bg2t)],
        out_specs=[row(nqk), row(nv), row(nv), row(nqk), row(ns), row(ns), row(ns),
                   col(nqk), col(nqk), col(nqk), col(ns)],
        out_shape=[sds((rows, nqk), F32), sds((rows, nv), F32), sds((rows, nv), F32), sds((rows, nqk), F32),
                   sds((rows, ns), BF16), sds((rows, ns), F32), sds((rows, ns), F32),
                   sds((nqk, rows), F32), sds((nqk, rows), F32), sds((nqk, rows), F32), sds((ns, rows), BF16)],
        compiler_params=_cparams(("arbitrary",)),
        name="in_proj",
    )(x, nw2, wn, wt, wglrt, wg2, bg2, wg2t, bg2t)


def _gla_out(o, gnw, r):
    o = o * lax.rsqrt(jnp.mean(o * o, axis=-1, keepdims=True) + NORM_EPS) * gnw
    return o * (r * _sigmoid(r))


def _gla_prompt_kernel(q_ref, v_ref, r_ref, la_ref, kt_ref, lat_ref, gnw_ref, o_ref, s_out_ref, s_ref):
    i = pl.program_id(0)
    c = GLA_CHUNK

    @pl.when(i == 0)
    def _():
        s_ref[...] = jnp.zeros_like(s_ref)

    rr = lax.broadcasted_iota(jnp.int32, (c, c), 0)
    cc = lax.broadcasted_iota(jnp.int32, (c, c), 1)
    causal = cc <= rr
    tri = jnp.where(causal, 1.0, 0.0).astype(BF16)
    trit = jnp.where(rr <= cc, 1.0, 0.0).astype(BF16)
    mid = c // 2

    for ch in range(GLA_TILE // c):
        rows = slice(ch * c, (ch + 1) * c)
        la_hi, la_lo = _split_bf16(la_ref[rows, :])
        b = _dot(tri, la_hi) + _dot(tri, la_lo)
        lat_hi, lat_lo = _split_bf16(lat_ref[:, rows])
        bt = _dot(lat_hi, trit) + _dot(lat_lo, trit)
        q = q_ref[rows, :]
        kt = kt_ref[:, rows]
        qe = q * jnp.exp(b)
        qm = q * jnp.exp(b - b[mid:mid + 1, :])
        ktm = kt * jnp.exp(bt[:, mid:mid + 1] - bt)
        ktd = kt * jnp.exp(bt[:, c - 1:c] - bt)
        dec = jnp.exp(bt[:, c - 1:c])
        for h in range(GLA_HEADS):
            ks = slice(h * GLA_DK, (h + 1) * GLA_DK)
            vs = slice(h * GLA_DV, (h + 1) * GLA_DV)
            s = s_ref[h]
            v = v_ref[rows, vs]
            att = jnp.where(causal, _dot(qm[:, ks], ktm[ks, :]), 0.0)
            o = _dot(qe[:, ks], s) + _dot(att, v)
            s_ref[h] = dec[ks, :] * s + _dot(ktd[ks, :], v)
            o_ref[rows, vs] = _gla_out(o, gnw_ref[h:h + 1, :], r_ref[rows, vs])

    @pl.when(i == pl.num_programs(0) - 1)
    def _():
        s_out_ref[...] = s_ref[...]


def _gla_prompt(qg, vg, rg, la, kgt, lat, gnw):
    t = qg.shape[0]
    nqk = GLA_HEADS * GLA_DK
    nv = GLA_HEADS * GLA_DV
    tg = GLA_TILE
    assert t % tg == 0
    row = lambda n: pl.BlockSpec((tg, n), lambda i: (i, 0))
    col = lambda n: pl.BlockSpec((n, tg), lambda i: (0, i))
    return pl.pallas_call(
        _gla_prompt_kernel,
        grid=(t // tg,),
        in_specs=[row(nqk), row(nv), row(nv), row(nqk), col(nqk), col(nqk),
                  pl.BlockSpec((GLA_HEADS, GLA_DV), lambda i: (0, 0))],
        out_specs=[row(nv), pl.BlockSpec((GLA_HEADS, GLA_DK, GLA_DV), lambda i: (0, 0, 0))],
        out_shape=[jax.ShapeDtypeStruct((t, nv), F32),
                   jax.ShapeDtypeStruct((GLA_HEADS, GLA_DK, GLA_DV), F32)],
        scratch_shapes=[pltpu.VMEM((GLA_HEADS, GLA_DK, GLA_DV), F32)],
        compiler_params=_cparams(("arbitrary",)),
        name="gla_prompt",
    )(qg, vg, rg, la, kgt, lat, gnw)


def _gla_sample_kernel(qt_ref, kt_ref, lat_ref, v_ref, r_ref, s0_ref, gnw_ref, o_ref, s_out_ref):
    n_t = v_ref.shape[1]
    for h in range(GLA_HEADS):
        ks = slice(h * GLA_DK, (h + 1) * GLA_DK)
        vs = slice(h * GLA_DV, (h + 1) * GLA_DV)
        s = s0_ref[0, h]
        for t in range(n_t):
            a = jnp.exp(lat_ref[0, ks, t:t + 1])
            s = a * s + kt_ref[0, ks, t:t + 1] * v_ref[0, t:t + 1, vs]
            o = jnp.sum(qt_ref[0, ks, t:t + 1] * s, axis=0, keepdims=True)
            o_ref[0, t:t + 1, vs] = _gla_out(o, gnw_ref[h:h + 1, :], r_ref[0, t:t + 1, vs])
        s_out_ref[0, h] = s


def _gla_sample(qgt, kgt, lat, vg, rg, s0, gnw):
    b, n_t, nv = vg.shape
    nqk = GLA_HEADS * GLA_DK
    colb = pl.BlockSpec((1, nqk, n_t), lambda i: (i, 0, 0))
    rowb = pl.BlockSpec((1, n_t, nv), lambda i: (i, 0, 0))
    st = pl.BlockSpec((1, GLA_HEADS, GLA_DK, GLA_DV), lambda i: (i, 0, 0, 0))
    return pl.pallas_call(
        _gla_sample_kernel,
        grid=(b,),
        in_specs=[colb, colb, colb, rowb, rowb, st, pl.BlockSpec((GLA_HEADS, GLA_DV), lambda i: (0, 0))],
        out_specs=[rowb, st],
        out_shape=[jax.ShapeDtypeStruct((b, n_t, nv), F32),
                   jax.ShapeDtypeStruct((b, GLA_HEADS, GLA_DK, GLA_DV), F32)],
        compiler_params=_cparams(("arbitrary",)),
        name="gla_sample",
    )(qgt, kgt, lat, vg, rg, s0, gnw)


def _neg_tri(n, dtype):
    rr = lax.broadcasted_iota(jnp.int32, (n, n), 0)
    cc = lax.broadcasted_iota(jnp.int32, (n, n), 1)
    return jnp.where(rr >= cc, -1.0, 0.0).astype(dtype)


def _softplus2(z2):
    sign_bit = jnp.uint32(0x80000000)
    neg_abs = pltpu.bitcast(pltpu.bitcast(z2, jnp.uint32) | sign_bit, F32)
    return jnp.maximum(z2, 0.0) + jnp.log2(1.0 + jnp.exp2(neg_abs))


def _sb_prompt_kernel(bias_ref, q_ref, kt_ref, v_ref, o_ref, z_sc, sp_sc, w_sc, *, blk, hp):
    g = pl.program_id(0)
    i = pl.program_id(1)
    ntri = _neg_tri(blk, BF16)

    def logits(hh, j):
        z = _dot(q_ref[hh], kt_ref[hh, j]) + bias_ref[g * hp + hh] * LOG2E
        return z, _softplus2(z)

    rr = lax.broadcasted_iota(jnp.int32, (blk, blk), 0)
    cc = lax.broadcasted_iota(jnp.int32, (blk, blk), 1)
    mask = cc < rr
    for hh in range(hp):
        z, sp = logits(hh, i)
        z_sc[hh] = jnp.where(mask, z, SB_MASKED_LOGIT)
        sp_sc[hh] = jnp.where(mask, sp, 0.0).astype(BF16)
        w_sc[hh] = jnp.zeros((blk, blk), BF16)

    def body(n, st):
        j_logits = jnp.maximum(i - n, 0)
        j_value = jnp.minimum(i - n + 2, i)
        out = []
        for hh in range(hp):
            carry, acc = st[2 * hh], st[2 * hh + 1]
            acc = acc + _dot(w_sc[hh], v_ref[hh, j_value])
            lsum = _dot(sp_sc[hh], ntri)
            w_sc[hh] = jnp.exp2(z_sc[hh] + lsum + carry).astype(BF16)
            carry = carry + lsum[:, 0:1]
            z, sp = logits(hh, j_logits)
            z_sc[hh] = z
            sp_sc[hh] = sp.astype(BF16)
            out.extend((carry, acc))
        return tuple(out)

    init = (jnp.zeros((blk, 1), F32), jnp.zeros((blk, SB_HD), F32)) * hp
    state = lax.fori_loop(1, i + 3, body, init)
    for hh in range(hp):
        o_ref[hh] = state[2 * hh + 1]


def _sb_prompt(qs, kst, vs, bias):
    t = qs.shape[0]
    blk = min(t, MXU_DIM)
    assert t % blk == 0
    nb = t // blk
    hp = SB_HEADS_PER_STEP
    q3 = qs.reshape(t, SB_HEADS, SB_HD).transpose(1, 0, 2)
    kt4 = kst.reshape(SB_HEADS, SB_HD, nb, blk).transpose(0, 2, 1, 3)
    v4 = vs.astype(BF16).reshape(nb, blk, SB_HEADS, SB_HD).transpose(2, 0, 1, 3)
    o3 = pl.pallas_call(
        functools.partial(_sb_prompt_kernel, blk=blk, hp=hp),
        grid=(SB_HEADS // hp, nb),
        in_specs=[pl.BlockSpec(memory_space=pltpu.SMEM),
                  pl.BlockSpec((hp, blk, SB_HD), lambda h, i: (h, i, 0)),
                  pl.BlockSpec((hp, nb, SB_HD, blk), lambda h, i: (h, 0, 0, 0), pipeline_mode=pl.Buffered(1)),
                  pl.BlockSpec((hp, nb, blk, SB_HD), lambda h, i: (h, 0, 0, 0), pipeline_mode=pl.Buffered(1))],
        out_specs=pl.BlockSpec((hp, blk, SB_HD), lambda h, i: (h, i, 0)),
        out_shape=jax.ShapeDtypeStruct((SB_HEADS, t, SB_HD), F32),
        scratch_shapes=[pltpu.VMEM((hp, blk, blk), F32), pltpu.VMEM((hp, blk, blk), BF16),
                        pltpu.VMEM((hp, blk, blk), BF16)],
        compiler_params=_cparams(("arbitrary", "arbitrary")),
        name="sb_prompt",
    )(bias, q3, kt4, v4)
    return o3.transpose(1, 0, 2).reshape(t, SB_HEADS * SB_HD)


def _sb_sample_kernel(pt_ref, q_ref, bias_ref, kn_ref, vn_ref, *rest, n_t, n_pg):
    k_refs = rest[:n_pg]
    v_refs = rest[n_pg:2 * n_pg]
    o_ref, carry_ref, acc_ref = rest[2 * n_pg:]
    p = pl.program_id(1)
    q = q_ref[0]
    nr, hd = q.shape
    bias = bias_ref[...]
    page = k_refs[0].shape[2]

    @pl.when(p == 0)
    def _():
        t_row = lax.broadcasted_iota(jnp.int32, (nr, 1), 0) // SB_HEADS
        carry = jnp.zeros((nr, 1), F32)
        acc = jnp.zeros((nr, hd), F32)
        for s in range(n_t - 2, -1, -1):
            z = jnp.sum(q * kn_ref[0, s:s + 1, :], axis=-1, keepdims=True) + bias
            m = s < t_row
            carry = carry - jnp.where(m, _softplus2(z), 0.0)
            w = jnp.where(m, jnp.exp2(z + carry), 0.0)
            acc = acc + w * vn_ref[0, s:s + 1, :]
        carry_ref[...] = carry
        acc_ref[...] = acc

    qb = q.astype(BF16)
    ntri = _neg_tri(page, BF16)
    zs = [_dot(qb, k_refs[g][0].astype(BF16)) + bias for g in range(n_pg)]
    ls = [_dot(_softplus2(z).astype(BF16), ntri) for z in zs]
    carry = carry_ref[...]
    acc = jnp.zeros((nr, hd), F32)
    for g in range(n_pg):
        w = jnp.exp2(zs[g] + ls[g] + carry)
        acc = acc + _dot_nt(w.astype(BF16), v_refs[g][0].astype(BF16))
        carry = carry + ls[g][:, 0:1]
    carry_ref[...] = carry
    acc_ref[...] += acc

    @pl.when(p == pl.num_programs(1) - 1)
    def _():
        row_head = lax.broadcasted_iota(jnp.int32, (SB_HEADS, hd), 0)
        lane_head = lax.broadcasted_iota(jnp.int32, (SB_HEADS, hd), 1) // SB_HD
        own = (row_head == lane_head)[None]
        a3 = acc_ref[...].reshape(n_t, SB_HEADS, hd)
        o_ref[0] = jnp.sum(jnp.where(own, a3, 0.0), axis=1)


def _sb_sample(qs, ks_new, vs_new, pool_kt, pool_vt, page_table, bias):
    b, n_t, hd = qs.shape
    n_pages = page_table.shape[1]
    page = pool_kt.shape[2]
    n_pg = SB_PAGES_PER_STEP
    while n_pages % n_pg:
        n_pg //= 2
    nr = SB_HEADS * n_t
    head_of_lane = jnp.arange(hd, dtype=jnp.int32) // SB_HD
    head_of_row = jnp.arange(nr, dtype=jnp.int32) % SB_HEADS
    sel = (head_of_row[:, None] == head_of_lane[None, :])
    q_rows = jnp.repeat(qs.astype(F32), SB_HEADS, axis=1)
    q_bd = jnp.where(sel[None], q_rows, 0.0)
    bias_col = (bias[head_of_row] * LOG2E).reshape(nr, 1)

    def page_spec(g):
        return pl.BlockSpec((1, hd, page), lambda i, p, pt: (pt[i, n_pages - 1 - (p * n_pg + g)], 0, 0))

    grid_spec = pltpu.PrefetchScalarGridSpec(
        num_scalar_prefetch=1,
        grid=(b, n_pages // n_pg),
        in_specs=[pl.BlockSpec((1, nr, hd), lambda i, p, pt: (i, 0, 0)),
                  pl.BlockSpec((nr, 1), lambda i, p, pt: (0, 0)),
                  pl.BlockSpec((1, n_t, hd), lambda i, p, pt: (i, 0, 0)),
                  pl.BlockSpec((1, n_t, hd), lambda i, p, pt: (i, 0, 0))]
                 + [page_spec(g) for g in range(n_pg)] * 2,
        out_specs=pl.BlockSpec((1, n_t, hd), lambda i, p, pt: (i, 0, 0)),
        scratch_shapes=[pltpu.VMEM((nr, 1), F32), pltpu.VMEM((nr, hd), F32)],
    )
    return pl.pallas_call(
        functools.partial(_sb_sample_kernel, n_t=n_t, n_pg=n_pg),
        grid_spec=grid_spec,
        out_shape=jax.ShapeDtypeStruct((b, n_t, hd), F32),
        compiler_params=_cparams(("arbitrary", "arbitrary")),
        name="sb_sample",
    )(page_table, q_bd, bias_col, ks_new, vs_new, *([pool_kt] * n_pg), *([pool_vt] * n_pg))


def _mix_xattn_kernel(x_ref, og_ref, os_ref, wo_ref, nw_ref, wcq_ref, mk_ref, mv_ref, wco_ref, o_ref):
    ng = og_ref.shape[2]
    x = x_ref[0]
    x = x + _dot(og_ref[0].astype(BF16), wo_ref[0:ng, :]) + _dot(os_ref[0].astype(BF16), wo_ref[ng:, :])
    h = _rms(x, nw_ref[...]).astype(BF16)
    d = x.shape[1]
    hd = d // X_HEADS
    q = _dot(h, wcq_ref[...]) * (hd ** -0.5)
    outs = []
    for hh in range(X_HEADS):
        sl = slice(hh * hd, (hh + 1) * hd)
        s = _dot_nt(q[:, sl], mk_ref[0, :, sl])
        s = s - jnp.max(s, axis=-1, keepdims=True)
        e = jnp.exp(s)
        p = e / jnp.sum(e, axis=-1, keepdims=True)
        outs.append(_dot(p, mv_ref[0, :, sl]))
    o = jnp.concatenate(outs, axis=-1).astype(BF16)
    o_ref[0] = x + _dot(o, wco_ref[...])


def _mix_xattn(x, og, osb, wo, nw, wcq, mk, mv, wco):
    b, r, d = x.shape
    m = mk.shape[1]
    tm = min(r, 512)
    assert r % tm == 0
    ng = og.shape[2]
    ns = osb.shape[2]
    const = lambda shape: pl.BlockSpec(shape, lambda i, j: (0, 0), pipeline_mode=pl.Buffered(1))
    row = lambda n: pl.BlockSpec((1, tm, n), lambda i, j: (i, j, 0))
    mem = pl.BlockSpec((1, m, d), lambda i, j: (i, 0, 0))
    return pl.pallas_call(
        _mix_xattn_kernel,
        grid=(b, r // tm),
        in_specs=[row(d), row(ng), row(ns), const((ng + ns, d)), const((1, d)), const((d, d)), mem, mem,
                  const((d, d))],
        out_specs=row(d),
        out_shape=jax.ShapeDtypeStruct((b, r, d), F32),
        compiler_params=_cparams(("arbitrary", "arbitrary")),
        name="mix_xattn",
    )(x, og, osb, wo, nw.reshape(1, d), wcq, mk, mv, wco)


def kernel(x_prompt, x_sample, mem_prompt, cache_sb_k, cache_sb_v, page_table, state_gla, cache_mem_k, cache_mem_v, ffn1_norm_w, ffn1_w_gate, ffn1_w_up, ffn1_w_down, mix_norm_w, w_in, w_gate2, b_gate2, gla_norm_w, sb_bias, w_out, xattn_norm_w, mem_norm_w, w_mk, w_mv, w_cq, w_co, ffn2_norm_w, ffn2_w_gate, ffn2_w_up, ffn2_w_down, final_norm_w):
    depth = w_in.shape[0]
    b_p, t_p, d = x_prompt.shape
    b_d, t_d, _ = x_sample.shape
    assert b_p == 1
    n_pool, page = cache_sb_k.shape[1], cache_sb_k.shape[2]
    hd_s = SB_HEADS * SB_HD
    nv = GLA_HEADS * GLA_DV

    xp = x_prompt.reshape(b_p * t_p, d)
    xs = x_sample.reshape(b_d * t_d, d)
    outs = [[] for _ in range(8)]
    for l in range(depth):
        last = l == depth - 1
        f1 = (_pad_ff(ffn1_w_gate[l], 1), _pad_ff(ffn1_w_up[l], 1), _pad_ff(ffn1_w_down[l], 0))
        f2 = (_pad_ff(ffn2_w_gate[l], 1), _pad_ff(ffn2_w_up[l], 1), _pad_ff(ffn2_w_down[l], 0))
        wproj = _prep_inproj_weights(w_in[l], w_gate2[l], b_gate2[l])
        wo = w_out[l].astype(BF16)
        wcq = w_cq[l].astype(BF16)
        wco = w_co[l].astype(BF16)

        mk_p, mv_p = _memory_kv(mem_prompt, mem_norm_w[l], w_mk[l], w_mv[l])

        x1 = _ffn(xp, ffn1_norm_w[l], *f1, final_norm_w, False)
        qg, vg, rg, la, qs, ks, vs, _, kgt, lat, kst = _inproj(x1, mix_norm_w[l], wproj)
        og, s_p = _gla_prompt(qg, vg, rg, la, kgt, lat, gla_norm_w[l])
        osb = _sb_prompt(qs, kst, vs, sb_bias[l])
        x3 = _mix_xattn(x1.reshape(b_p, t_p, d), og.reshape(b_p, t_p, nv), osb.reshape(b_p, t_p, hd_s),
                        wo, xattn_norm_w[l], wcq, mk_p, mv_p, wco)
        xp = _ffn(x3.reshape(b_p * t_p, d), ffn2_norm_w[l], *f2, final_norm_w, last)
        outs[0].append(ks.reshape(b_p, t_p, SB_HEADS, SB_HD))
        outs[1].append(vs.reshape(b_p, t_p, SB_HEADS, SB_HD))
        outs[2].append(s_p.reshape(b_p, GLA_HEADS, GLA_DK, GLA_DV))
        outs[3].append(mk_p.reshape(b_p, -1, X_HEADS, d // X_HEADS))
        outs[4].append(mv_p.reshape(b_p, -1, X_HEADS, d // X_HEADS))

        x1 = _ffn(xs, ffn1_norm_w[l], *f1, final_norm_w, False)
        _, vg, rg, _, qs, ks, vs, qgt, kgt, lat, _ = _inproj(x1, mix_norm_w[l], wproj)
        tok_major = lambda a: a.reshape(a.shape[0], b_d, t_d).transpose(1, 0, 2)
        feat_major = lambda c: c.transpose(0, 2, 3, 1).reshape(n_pool, hd_s, page)
        og, s_s = _gla_sample(tok_major(qgt), tok_major(kgt), tok_major(lat),
                              vg.reshape(b_d, t_d, nv), rg.reshape(b_d, t_d, nv), state_gla[l], gla_norm_w[l])
        osb = _sb_sample(qs.reshape(b_d, t_d, hd_s), ks.reshape(b_d, t_d, hd_s), vs.reshape(b_d, t_d, hd_s),
                         feat_major(cache_sb_k[l]), feat_major(cache_sb_v[l]), page_table, sb_bias[l])
        m_len = cache_mem_k.shape[2]
        x3 = _mix_xattn(x1.reshape(b_d, t_d, d), og, osb, wo, xattn_norm_w[l], wcq,
                        cache_mem_k[l].reshape(b_d, m_len, d), cache_mem_v[l].reshape(b_d, m_len, d), wco)
        xs = _ffn(x3.reshape(b_d * t_d, d), ffn2_norm_w[l], *f2, final_norm_w, last)
        outs[5].append(ks.reshape(b_d, t_d, SB_HEADS, SB_HD))
        outs[6].append(vs.reshape(b_d, t_d, SB_HEADS, SB_HD))
        outs[7].append(s_s)

    stk = [jnp.stack(o) for o in outs]
    return (xp.reshape(b_p, t_p, d), xs.reshape(b_d, t_d, d),
            stk[0], stk[1], stk[2], stk[3], stk[4], stk[5], stk[6], stk[7])
```

```python
import functools

import jax
import jax.numpy as jnp
from jax import lax
from jax.experimental import pallas as pl
from jax.experimental.pallas import tpu as pltpu

F32 = jnp.float32
BF16 = jnp.bfloat16

NORM_EPS = 1e-6
FFN_RES = 0.5
GLA_HEADS = 4
GLA_DK = 64
GLA_DV = 128
GLA_GATE_RANK = 16
GLA_GATE_TAU = 16.0
GLA_CHUNK = 64
GLA_TILE = 128
SB_HEADS = 8
SB_HD = 64
SB_MASKED_LOGIT = -1e30
SB_AUG = 16
SB_AUG_ONES = 3
SB_STEPS_PER_TRIP = 4
SB_HEADS_PER_STEP = 4
SB_PAGES_PER_STEP = 8
LOG2E = 1.4426950408889634
X_HEADS = 4
LANE = 128
MXU_DIM = 256
VMEM_LIMIT = 56 * 1024 * 1024


def _cparams(sem):
    return pltpu.CompilerParams(dimension_semantics=sem, vmem_limit_bytes=VMEM_LIMIT)


def _rms(x, w):
    return x * lax.rsqrt(jnp.mean(x * x, axis=-1, keepdims=True) + NORM_EPS) * w


def _softplus(z):
    return jnp.maximum(z, 0.0) + jnp.log(1.0 + jnp.exp(-jnp.abs(z)))


def _log_sigmoid(z):
    return -_softplus(-z)


def _sigmoid(z):
    return 1.0 / (1.0 + jnp.exp(-z))


def _dot(a, b):
    return jnp.dot(a, b, preferred_element_type=F32)


def _dot_nt(a, b):
    return lax.dot_general(a, b, (((1,), (1,)), ((), ())), preferred_element_type=F32)


def _split_bf16(x):
    hi = x.astype(BF16)
    lo = (x - hi.astype(F32)).astype(BF16)
    return hi, lo


def _memkv_kernel(mem_ref, nw_ref, wk_ref, wv_ref, k_ref, v_ref):
    mn = _rms(mem_ref[0], nw_ref[...]).astype(BF16)
    k_ref[0] = _dot(mn, wk_ref[...])
    v_ref[0] = _dot(mn, wv_ref[...])


def _memory_kv(mem, nw, wk, wv):
    b, m, d = mem.shape
    full = lambda shape: pl.BlockSpec(shape, lambda i: (0,) * len(shape))
    row = pl.BlockSpec((1, m, d), lambda i: (i, 0, 0))
    return pl.pallas_call(
        _memkv_kernel,
        grid=(b,),
        in_specs=[row, full((1, d)), full((d, d)), full((d, d))],
        out_specs=[row, row],
        out_shape=[jax.ShapeDtypeStruct((b, m, d), F32)] * 2,
        compiler_params=_cparams(("arbitrary",)),
        name="memory_kv",
    )(mem, nw.reshape(1, d), wk.astype(BF16), wv.astype(BF16))


def _ffn_kernel(x_ref, nw_ref, wg_ref, wu_ref, wd_ref, fw_ref, o_ref, *, final_norm):
    x = x_ref[...]
    h = _rms(x, nw_ref[...]).astype(BF16)
    acc = jnp.zeros_like(x)
    for c in range(wg_ref.shape[1] // MXU_DIM):
        sl = slice(c * MXU_DIM, (c + 1) * MXU_DIM)
        g = _dot(h, wg_ref[:, sl])
        u = _dot(h, wu_ref[:, sl])
        a = (g * _sigmoid(g) * u).astype(BF16)
        acc = acc + _dot(a, wd_ref[sl, :])
    y = x + FFN_RES * acc
    if final_norm:
        y = _rms(y, fw_ref[...])
    o_ref[...] = y


def _pad_ff(w, axis):
    ff = w.shape[axis]
    pad = (-ff) % MXU_DIM
    cfg = [(0, 0), (0, 0)]
    cfg[axis] = (0, pad)
    return jnp.pad(w, cfg).astype(BF16)


def _ffn(x, nw, wg, wu, wd, fw, final_norm):
    rows, d = x.shape
    tm = min(rows, 512)
    assert rows % tm == 0
    ffp = wg.shape[1]
    const = lambda shape: pl.BlockSpec(shape, lambda i: (0, 0), pipeline_mode=pl.Buffered(1))
    row = pl.BlockSpec((tm, d), lambda i: (i, 0))
    return pl.pallas_call(
        functools.partial(_ffn_kernel, final_norm=final_norm),
        grid=(rows // tm,),
        in_specs=[row, const((1, d)), const((d, ffp)), const((d, ffp)), const((ffp, d)), const((1, d))],
        out_specs=row,
        out_shape=jax.ShapeDtypeStruct((rows, d), F32),
        compiler_params=_cparams(("arbitrary",)),
        name="ffn_final" if final_norm else "ffn",
    )(x, nw.reshape(1, d), wg, wu, wd, fw.reshape(1, d))


NQK = GLA_HEADS * GLA_DK
NV = GLA_HEADS * GLA_DV
NS = SB_HEADS * SB_HD


def _gate_log_decay(pre):
    return _log_sigmoid(pre) * (1.0 / GLA_GATE_TAU)


def _inproj_prompt_kernel(x_ref, nw_ref, wn_ref, wt_ref, wglrt_ref, wg2_ref, bg2_ref, wg2t_ref, bg2t_ref,
                          qg_ref, vg_ref, rg_ref, la_ref, qs_ref,
                          kgt_ref, lat_ref, kst_ref, vst_ref, kstb_ref, vstb_ref):
    h = _rms(x_ref[...], nw_ref[...]).astype(BF16)
    y = _dot(h, wn_ref[...])
    qg_ref[...] = y[:, 0:NQK] * (GLA_DK ** -0.5)
    vg_ref[...] = y[:, NQK:NQK + NV]
    rg_ref[...] = y[:, NQK + NV:NQK + 2 * NV]
    o = NQK + 2 * NV
    qs_ref[...] = (y[:, o:o + NS] * (SB_HD ** -0.5 * LOG2E)).astype(BF16)
    glr = y[:, o + NS:o + NS + LANE].astype(BF16)
    la_ref[...] = _gate_log_decay(_dot(glr, wg2_ref[...]) + bg2_ref[...])
    yt = _dot_nt(wt_ref[...], h)
    kgt_ref[...] = yt[0:NQK]
    kst = yt[NQK:NQK + NS]
    vst = yt[NQK + NS:NQK + 2 * NS]
    kst_ref[...] = kst
    vst_ref[...] = vst
    aug_row = lax.broadcasted_iota(jnp.int32, (SB_AUG, kst.shape[1]), 0)
    ones_rows = jnp.where(aug_row < SB_AUG_ONES, 1.0, 0.0).astype(BF16)
    for hh in range(SB_HEADS):
        base = hh * (SB_HD + SB_AUG)
        kstb_ref[base:base + SB_HD, :] = kst[hh * SB_HD:(hh + 1) * SB_HD].astype(BF16)
        kstb_ref[base + SB_HD:base + SB_HD + SB_AUG, :] = ones_rows
    vstb_ref[...] = vst.astype(BF16)
    glrt = _dot_nt(wglrt_ref[...], h).astype(BF16)
    lat_ref[...] = _gate_log_decay(_dot(wg2t_ref[...], glrt) + bg2t_ref[...])


def _inproj_sample_kernel(x_ref, nw_ref, wn_ref, wt_ref, wglrt_ref, wg2t_ref, bg2t_ref,
                          vg_ref, rg_ref, qs_ref, ks_ref, vs_ref, qgt_ref, kgt_ref, lat_ref):
    h = _rms(x_ref[...], nw_ref[...]).astype(BF16)
    y = _dot(h, wn_ref[...])
    vg_ref[...] = y[:, 0:NV]
    rg_ref[...] = y[:, NV:2 * NV]
    o = 2 * NV
    qs_ref[...] = (y[:, o:o + NS] * (SB_HD ** -0.5 * LOG2E)).astype(BF16)
    ks_ref[...] = y[:, o + NS:o + 2 * NS]
    vs_ref[...] = y[:, o + 2 * NS:o + 3 * NS]
    yt = _dot_nt(wt_ref[...], h)
    qgt_ref[...] = yt[0:NQK] * (GLA_DK ** -0.5)
    kgt_ref[...] = yt[NQK:2 * NQK]
    glrt = _dot_nt(wglrt_ref[...], h).astype(BF16)
    lat_ref[...] = _gate_log_decay(_dot(wg2t_ref[...], glrt) + bg2t_ref[...])


def _prep_inproj_weights(w_in, w_gate2, b_gate2):
    sizes = (NQK, NQK, NV, GLA_GATE_RANK, NV, NS, NS, NS)
    offs = [0]
    for s in sizes:
        offs.append(offs[-1] + s)
    qg, kg, vg, glr, rg, qs, ks, vs = [w_in[:, offs[i]:offs[i + 1]] for i in range(8)]
    glr_pad = jnp.pad(glr, ((0, 0), (0, LANE - GLA_GATE_RANK)))
    cat = lambda parts: jnp.concatenate(parts, axis=1)
    shared = dict(
        wglrt=glr.T.astype(BF16),
        wg2=jnp.pad(w_gate2, ((0, LANE - GLA_GATE_RANK), (0, 0))).astype(BF16),
        bg2=b_gate2.reshape(1, NQK),
        wg2t=w_gate2.T.astype(BF16),
        bg2t=b_gate2.reshape(NQK, 1))
    prompt = dict(wn=cat([qg, vg, rg, qs, glr_pad]).astype(BF16), wt=cat([kg, ks, vs]).T.astype(BF16), **shared)
    sample = dict(wn=cat([vg, rg, qs, ks, vs]).astype(BF16), wt=cat([qg, kg]).T.astype(BF16), **shared)
    return prompt, sample


def _inproj_call(kernel_fn, name, x, inputs, row_outs, col_outs):
    rows, d = x.shape
    tm = min(rows, 256)
    assert rows % tm == 0
    const = lambda a: pl.BlockSpec(a.shape, lambda i: (0, 0), pipeline_mode=pl.Buffered(1))
    sds = jax.ShapeDtypeStruct
    return pl.pallas_call(
        kernel_fn,
        grid=(rows // tm,),
        in_specs=[pl.BlockSpec((tm, d), lambda i: (i, 0))] + [const(a) for a in inputs],
        out_specs=[pl.BlockSpec((tm, n), lambda i: (i, 0)) for n, _ in row_outs]
                  + [pl.BlockSpec((n, tm), lambda i: (0, i)) for n, _ in col_outs],
        out_shape=[sds((rows, n), dt) for n, dt in row_outs] + [sds((n, rows), dt) for n, dt in col_outs],
        compiler_params=_cparams(("arbitrary",)),
        name=name,
    )(x, *inputs)


def _inproj_prompt(x, nw, w):
    inputs = [nw.reshape(1, -1), w["wn"], w["wt"], w["wglrt"], w["wg2"], w["bg2"], w["wg2t"], w["bg2t"]]
    return _inproj_call(_inproj_prompt_kernel, "in_proj_prompt", x, inputs,
                        [(NQK, F32), (NV, F32), (NV, F32), (NQK, F32), (NS, BF16)],
                        [(NQK, F32), (NQK, F32), (NS, F32), (NS, F32), (SB_HEADS * (SB_HD + SB_AUG), BF16),
                         (NS, BF16)])


def _inproj_sample(x, nw, w):
    inputs = [nw.reshape(1, -1), w["wn"], w["wt"], w["wglrt"], w["wg2t"], w["bg2t"]]
    return _inproj_call(_inproj_sample_kernel, "in_proj_sample", x, inputs,
                        [(NV, F32), (NV, F32), (NS, BF16), (NS, F32), (NS, F32)],
                        [(NQK, F32), (NQK, F32), (NQK, F32)])


def _gla_out(o, gnw, r):
    o = o * lax.rsqrt(jnp.mean(o * o, axis=-1, keepdims=True) + NORM_EPS) * gnw
    return o * (r * _sigmoid(r))


def _gla_prompt_kernel(q_ref, v_ref, r_ref, la_ref, kt_ref, lat_ref, gnw_ref, o_ref, s_out_ref, s_ref):
    i = pl.program_id(0)
    c = GLA_CHUNK

    @pl.when(i == 0)
    def _():
        s_ref[...] = jnp.zeros_like(s_ref)

    rr = lax.broadcasted_iota(jnp.int32, (c, c), 0)
    cc = lax.broadcasted_iota(jnp.int32, (c, c), 1)
    causal = cc <= rr
    tri = jnp.where(causal, 1.0, 0.0).astype(BF16)
    trit = jnp.where(rr <= cc, 1.0, 0.0).astype(BF16)
    mid = c // 2

    for ch in range(GLA_TILE // c):
        rows = slice(ch * c, (ch + 1) * c)
        la_hi, la_lo = _split_bf16(la_ref[rows, :])
        b = _dot(tri, la_hi) + _dot(tri, la_lo)
        lat_hi, lat_lo = _split_bf16(lat_ref[:, rows])
        bt = _dot(lat_hi, trit) + _dot(lat_lo, trit)
        q = q_ref[rows, :]
        kt = kt_ref[:, rows]
        qe = q * jnp.exp(b)
        qm = q * jnp.exp(b - b[mid:mid + 1, :])
        ktm = kt * jnp.exp(bt[:, mid:mid + 1] - bt)
        ktd = kt * jnp.exp(bt[:, c - 1:c] - bt)
        dec = jnp.exp(bt[:, c - 1:c])
        for h in range(GLA_HEADS):
            ks = slice(h * GLA_DK, (h + 1) * GLA_DK)
            vs = slice(h * GLA_DV, (h + 1) * GLA_DV)
            s = s_ref[h]
            v = v_ref[rows, vs]
            att = jnp.where(causal, _dot(qm[:, ks], ktm[ks, :]), 0.0)
            o = _dot(qe[:, ks], s) + _dot(att, v)
            s_ref[h] = dec[ks, :] * s + _dot(ktd[ks, :], v)
            o_ref[rows, vs] = _gla_out(o, gnw_ref[h:h + 1, :], r_ref[rows, vs])

    @pl.when(i == pl.num_programs(0) - 1)
    def _():
        s_out_ref[...] = s_ref[...]


def _gla_prompt(qg, vg, rg, la, kgt, lat, gnw):
    t = qg.shape[0]
    nqk = GLA_HEADS * GLA_DK
    nv = GLA_HEADS * GLA_DV
    tg = GLA_TILE
    assert t % tg == 0
    row = lambda n: pl.BlockSpec((tg, n), lambda i: (i, 0))
    col = lambda n: pl.BlockSpec((n, tg), lambda i: (0, i))
    return pl.pallas_call(
        _gla_prompt_kernel,
        grid=(t // tg,),
        in_specs=[row(nqk), row(nv), row(nv), row(nqk), col(nqk), col(nqk),
                  pl.BlockSpec((GLA_HEADS, GLA_DV), lambda i: (0, 0))],
        out_specs=[row(nv), pl.BlockSpec((GLA_HEADS, GLA_DK, GLA_DV), lambda i: (0, 0, 0))],
        out_shape=[jax.ShapeDtypeStruct((t, nv), F32),
                   jax.ShapeDtypeStruct((GLA_HEADS, GLA_DK, GLA_DV), F32)],
        scratch_shapes=[pltpu.VMEM((GLA_HEADS, GLA_DK, GLA_DV), F32)],
        compiler_params=_cparams(("arbitrary",)),
        name="gla_prompt",
    )(qg, vg, rg, la, kgt, lat, gnw)


def _gla_sample_kernel(qt_ref, kt_ref, lat_ref, v_ref, r_ref, s0_ref, gnw_ref, o_ref, s_out_ref):
    n_t = v_ref.shape[1]
    for h in range(GLA_HEADS):
        ks = slice(h * GLA_DK, (h + 1) * GLA_DK)
        vs = slice(h * GLA_DV, (h + 1) * GLA_DV)
        s = s0_ref[0, h]
        for t in range(n_t):
            a = jnp.exp(lat_ref[0, ks, t:t + 1])
            s = a * s + kt_ref[0, ks, t:t + 1] * v_ref[0, t:t + 1, vs]
            o = jnp.sum(qt_ref[0, ks, t:t + 1] * s, axis=0, keepdims=True)
            o_ref[0, t:t + 1, vs] = _gla_out(o, gnw_ref[h:h + 1, :], r_ref[0, t:t + 1, vs])
        s_out_ref[0, h] = s


def _gla_sample(qgt, kgt, lat, vg, rg, s0, gnw):
    b, n_t, nv = vg.shape
    nqk = GLA_HEADS * GLA_DK
    colb = pl.BlockSpec((1, nqk, n_t), lambda i: (i, 0, 0))
    rowb = pl.BlockSpec((1, n_t, nv), lambda i: (i, 0, 0))
    st = pl.BlockSpec((1, GLA_HEADS, GLA_DK, GLA_DV), lambda i: (i, 0, 0, 0))
    return pl.pallas_call(
        _gla_sample_kernel,
        grid=(b,),
        in_specs=[colb, colb, colb, rowb, rowb, st, pl.BlockSpec((GLA_HEADS, GLA_DV), lambda i: (0, 0))],
        out_specs=[rowb, st],
        out_shape=[jax.ShapeDtypeStruct((b, n_t, nv), F32),
                   jax.ShapeDtypeStruct((b, GLA_HEADS, GLA_DK, GLA_DV), F32)],
        compiler_params=_cparams(("arbitrary",)),
        name="gla_sample",
    )(qgt, kgt, lat, vg, rg, s0, gnw)


def _neg_tri(n, dtype):
    rr = lax.broadcasted_iota(jnp.int32, (n, n), 0)
    cc = lax.broadcasted_iota(jnp.int32, (n, n), 1)
    return jnp.where(rr >= cc, -1.0, 0.0).astype(dtype)


def _softplus2(z2):
    sign_bit = jnp.uint32(0x80000000)
    neg_abs = pltpu.bitcast(pltpu.bitcast(z2, jnp.uint32) | sign_bit, F32)
    return jnp.maximum(z2, 0.0) + jnp.log2(1.0 + jnp.exp2(neg_abs))


def _sb_prompt_kernel(bias_ref, q_ref, kt_ref, vt_ref, o_ref, q_sc, z_sc, sp_sc, w_sc, *, blk, hp):
    g = pl.program_id(0)
    i = pl.program_id(1)
    ntri = _neg_tri(blk, BF16)
    aug_col = lax.broadcasted_iota(jnp.int32, (blk, SB_AUG), 1)
    for hh in range(hp):
        b2 = jnp.full((blk, SB_AUG), bias_ref[g * hp + hh] * LOG2E, F32)
        b_hi = b2.astype(BF16).astype(F32)
        bias_cols = jnp.where(aug_col == 0, b_hi, jnp.where(aug_col == 1, b2 - b_hi, 0.0))
        for masked in range(2):
            q_sc[masked, hh, :, 0:SB_HD] = q_ref[:, hh * SB_HD:(hh + 1) * SB_HD]
            mask_val = SB_MASKED_LOGIT if masked else 0.0
            q_sc[masked, hh, :, SB_HD:SB_HD + SB_AUG] = jnp.where(aug_col == 2, mask_val, bias_cols).astype(BF16)

    def key_block(ref, hh, j):
        return ref[hh, :, pl.ds(pl.multiple_of(j * blk, blk), blk)]

    def logits(hh, j, masked):
        z = _dot(q_sc[masked, hh], key_block(kt_ref, hh, j))
        return z, _softplus2(z)

    rr = lax.broadcasted_iota(jnp.int32, (blk, blk), 0)
    cc = lax.broadcasted_iota(jnp.int32, (blk, blk), 1)
    mask = cc < rr
    for hh in range(hp):
        z, sp = logits(hh, i, 0)
        z_sc[0, hh] = jnp.where(mask, z, SB_MASKED_LOGIT)
        sp_sc[0, hh] = jnp.where(mask, sp, 0.0).astype(BF16)
        w_sc[0, hh] = jnp.zeros((blk, blk), BF16)

    def step(n, slot, st):
        prev = 1 - slot
        j_logits = jnp.maximum(i - n, 0)
        j_value = jnp.clip(i - n + 2, 0, i)
        masked = jnp.where(n <= i, 0, 1)
        out = []
        for hh in range(hp):
            carry, acc = st[2 * hh], st[2 * hh + 1]
            acc = acc + _dot_nt(w_sc[prev, hh], key_block(vt_ref, hh, j_value))
            lsum = _dot(sp_sc[prev, hh], ntri)
            w_sc[slot, hh] = jnp.exp2(z_sc[prev, hh] + lsum + carry).astype(BF16)
            carry = carry + lsum[:, 0:1]
            z, sp = logits(hh, j_logits, masked)
            z_sc[slot, hh] = z
            sp_sc[slot, hh] = sp.astype(BF16)
            out.extend((carry, acc))
        return tuple(out)

    spt = SB_STEPS_PER_TRIP

    def body(p, st):
        for u in range(spt):
            st = step(spt * p + 1 + u, (1 + u) % 2, st)
        return st

    init = (jnp.zeros((blk, 1), F32), jnp.zeros((blk, SB_HD), F32)) * hp
    state = lax.fori_loop(0, (i + 2 + spt - 1) // spt, body, init)
    for hh in range(hp):
        o_ref[:, hh * SB_HD:(hh + 1) * SB_HD] = state[2 * hh + 1]


def _sb_prompt(qs, kst, vst, bias):
    t = qs.shape[0]
    blk = min(t, MXU_DIM)
    assert t % blk == 0
    hp = SB_HEADS_PER_STEP
    dk = SB_HD + SB_AUG
    whole_seq = lambda rows_per_head: pl.BlockSpec((hp, rows_per_head, t), lambda h, i: (h, 0, 0),
                                                   pipeline_mode=pl.Buffered(1))
    rows = pl.BlockSpec((blk, hp * SB_HD), lambda h, i: (i, h))
    return pl.pallas_call(
        functools.partial(_sb_prompt_kernel, blk=blk, hp=hp),
        grid=(SB_HEADS // hp, t // blk),
        in_specs=[pl.BlockSpec(memory_space=pltpu.SMEM), rows, whole_seq(dk), whole_seq(SB_HD)],
        out_specs=rows,
        out_shape=jax.ShapeDtypeStruct((t, SB_HEADS * SB_HD), F32),
        scratch_shapes=[pltpu.VMEM((2, hp, blk, dk), BF16), pltpu.VMEM((2, hp, blk, blk), F32),
                        pltpu.VMEM((2, hp, blk, blk), BF16), pltpu.VMEM((2, hp, blk, blk), BF16)],
        compiler_params=_cparams(("arbitrary", "arbitrary")),
        name="sb_prompt",
    )(bias, qs, kst.reshape(SB_HEADS, dk, t), vst.reshape(SB_HEADS, SB_HD, t))


def _sb_sample_kernel(pt_ref, q_ref, bias_ref, kn_ref, vn_ref, *rest, n_t, n_pg):
    k_refs = rest[:n_pg]
    v_refs = rest[n_pg:2 * n_pg]
    o_ref, carry_ref, acc_ref = rest[2 * n_pg:]
    p = pl.program_id(1)
    q = q_ref[0]
    nr, hd = q.shape
    bias = bias_ref[...]
    page = k_refs[0].shape[2]

    @pl.when(p == 0)
    def _():
        t_row = lax.broadcasted_iota(jnp.int32, (nr, 1), 0) // SB_HEADS
        carry = jnp.zeros((nr, 1), F32)
        acc = jnp.zeros((nr, hd), F32)
        for s in range(n_t - 2, -1, -1):
            z = jnp.sum(q * kn_ref[0, s:s + 1, :], axis=-1, keepdims=True) + bias
            m = s < t_row
            carry = carry - jnp.where(m, _softplus2(z), 0.0)
            w = jnp.where(m, jnp.exp2(z + carry), 0.0)
            acc = acc + w * vn_ref[0, s:s + 1, :]
        carry_ref[...] = carry
        acc_ref[...] = acc

    qb = q.astype(BF16)
    ntri = _neg_tri(page, BF16)
    zs = [_dot(qb, k_refs[g][0].astype(BF16)) + bias for g in range(n_pg)]
    ls = [_dot(_softplus2(z).astype(BF16), ntri) for z in zs]
    carry = carry_ref[...]
    acc = jnp.zeros((nr, hd), F32)
    for g in range(n_pg):
        w = jnp.exp2(zs[g] + ls[g] + carry)
        acc = acc + _dot_nt(w.astype(BF16), v_refs[g][0].astype(BF16))
        carry = carry + ls[g][:, 0:1]
    carry_ref[...] = carry
    acc_ref[...] += acc

    @pl.when(p == pl.num_programs(1) - 1)
    def _():
        row_head = lax.broadcasted_iota(jnp.int32, (SB_HEADS, hd), 0)
        lane_head = lax.broadcasted_iota(jnp.int32, (SB_HEADS, hd), 1) // SB_HD
        own = (row_head == lane_head)[None]
        a3 = acc_ref[...].reshape(n_t, SB_HEADS, hd)
        o_ref[0] = jnp.sum(jnp.where(own, a3, 0.0), axis=1)


def _sb_sample(qs, ks_new, vs_new, pool_kt, pool_vt, page_table, bias):
    b, n_t, hd = qs.shape
    n_pages = page_table.shape[1]
    page = pool_kt.shape[2]
    n_pg = SB_PAGES_PER_STEP
    while n_pages % n_pg:
        n_pg //= 2
    nr = SB_HEADS * n_t
    head_of_lane = jnp.arange(hd, dtype=jnp.int32) // SB_HD
    head_of_row = jnp.arange(nr, dtype=jnp.int32) % SB_HEADS
    sel = (head_of_row[:, None] == head_of_lane[None, :])
    q_rows = jnp.repeat(qs.astype(F32), SB_HEADS, axis=1)
    q_bd = jnp.where(sel[None], q_rows, 0.0)
    bias_col = (bias[head_of_row] * LOG2E).reshape(nr, 1)

    def page_spec(g):
        return pl.BlockSpec((1, hd, page), lambda i, p, pt: (pt[i, n_pages - 1 - (p * n_pg + g)], 0, 0))

    grid_spec = pltpu.PrefetchScalarGridSpec(
        num_scalar_prefetch=1,
        grid=(b, n_pages // n_pg),
        in_specs=[pl.BlockSpec((1, nr, hd), lambda i, p, pt: (i, 0, 0)),
                  pl.BlockSpec((nr, 1), lambda i, p, pt: (0, 0)),
                  pl.BlockSpec((1, n_t, hd), lambda i, p, pt: (i, 0, 0)),
                  pl.BlockSpec((1, n_t, hd), lambda i, p, pt: (i, 0, 0))]
                 + [page_spec(g) for g in range(n_pg)] * 2,
        out_specs=pl.BlockSpec((1, n_t, hd), lambda i, p, pt: (i, 0, 0)),
        scratch_shapes=[pltpu.VMEM((nr, 1), F32), pltpu.VMEM((nr, hd), F32)],
    )
    return pl.pallas_call(
        functools.partial(_sb_sample_kernel, n_t=n_t, n_pg=n_pg),
        grid_spec=grid_spec,
        out_shape=jax.ShapeDtypeStruct((b, n_t, hd), F32),
        compiler_params=_cparams(("arbitrary", "arbitrary")),
        name="sb_sample",
    )(page_table, q_bd, bias_col, ks_new, vs_new, *([pool_kt] * n_pg), *([pool_vt] * n_pg))


def _mix_xattn_kernel(x_ref, og_ref, os_ref, wo_ref, nw_ref, wcq_ref, mk_ref, mv_ref, wco_ref, o_ref):
    ng = og_ref.shape[2]
    x = x_ref[0]
    x = x + _dot(og_ref[0].astype(BF16), wo_ref[0:ng, :]) + _dot(os_ref[0].astype(BF16), wo_ref[ng:, :])
    h = _rms(x, nw_ref[...]).astype(BF16)
    d = x.shape[1]
    hd = d // X_HEADS
    q = _dot(h, wcq_ref[...]) * (hd ** -0.5)
    outs = []
    for hh in range(X_HEADS):
        sl = slice(hh * hd, (hh + 1) * hd)
        mk = mk_ref[0, :, sl] if len(mk_ref.shape) == 3 else mk_ref[0, :, hh, :]
        mv = mv_ref[0, :, sl] if len(mv_ref.shape) == 3 else mv_ref[0, :, hh, :]
        s = _dot_nt(q[:, sl], mk)
        s = s - jnp.max(s, axis=-1, keepdims=True)
        e = jnp.exp(s)
        p = e / jnp.sum(e, axis=-1, keepdims=True)
        outs.append(_dot(p, mv))
    o = jnp.concatenate(outs, axis=-1).astype(BF16)
    o_ref[0] = x + _dot(o, wco_ref[...])


def _mix_xattn(x, og, osb, wo, nw, wcq, mk, mv, wco):
    b, r, d = x.shape
    tm = min(r, 512)
    assert r % tm == 0
    ng = og.shape[2]
    ns = osb.shape[2]
    const = lambda shape: pl.BlockSpec(shape, lambda i, j: (0, 0), pipeline_mode=pl.Buffered(1))
    row = lambda n: pl.BlockSpec((1, tm, n), lambda i, j: (i, j, 0))
    mem = pl.BlockSpec((1,) + mk.shape[1:], lambda i, j: (i,) + (0,) * (mk.ndim - 1))
    return pl.pallas_call(
        _mix_xattn_kernel,
        grid=(b, r // tm),
        in_specs=[row(d), row(ng), row(ns), const((ng + ns, d)), const((1, d)), const((d, d)), mem, mem,
                  const((d, d))],
        out_specs=row(d),
        out_shape=jax.ShapeDtypeStruct((b, r, d), F32),
        compiler_params=_cparams(("arbitrary", "arbitrary")),
        name="mix_xattn",
    )(x, og, osb, wo, nw.reshape(1, d), wcq, mk, mv, wco)


def kernel(x_prompt, x_sample, mem_prompt, cache_sb_k, cache_sb_v, page_table, state_gla, cache_mem_k, cache_mem_v, ffn1_norm_w, ffn1_w_gate, ffn1_w_up, ffn1_w_down, mix_norm_w, w_in, w_gate2, b_gate2, gla_norm_w, sb_bias, w_out, xattn_norm_w, mem_norm_w, w_mk, w_mv, w_cq, w_co, ffn2_norm_w, ffn2_w_gate, ffn2_w_up, ffn2_w_down, final_norm_w):
    depth = w_in.shape[0]
    b_p, t_p, d = x_prompt.shape
    b_d, t_d, _ = x_sample.shape
    assert b_p == 1
    n_pool, page = cache_sb_k.shape[1], cache_sb_k.shape[2]
    hd_s = SB_HEADS * SB_HD
    nv = GLA_HEADS * GLA_DV

    xp = x_prompt.reshape(b_p * t_p, d)
    xs = x_sample.reshape(b_d * t_d, d)
    outs = [[] for _ in range(8)]
    for l in range(depth):
        last = l == depth - 1
        f1 = (_pad_ff(ffn1_w_gate[l], 1), _pad_ff(ffn1_w_up[l], 1), _pad_ff(ffn1_w_down[l], 0))
        f2 = (_pad_ff(ffn2_w_gate[l], 1), _pad_ff(ffn2_w_up[l], 1), _pad_ff(ffn2_w_down[l], 0))
        wproj_p, wproj_s = _prep_inproj_weights(w_in[l], w_gate2[l], b_gate2[l])
        wo = w_out[l].astype(BF16)
        wcq = w_cq[l].astype(BF16)
        wco = w_co[l].astype(BF16)

        mk_p, mv_p = _memory_kv(mem_prompt, mem_norm_w[l], w_mk[l], w_mv[l])

        x1 = _ffn(xp, ffn1_norm_w[l], *f1, final_norm_w, False)
        qg, vg, rg, la, qs, kgt, lat, kst, vst, kstb, vstb = _inproj_prompt(x1, mix_norm_w[l], wproj_p)
        og, s_p = _gla_prompt(qg, vg, rg, la, kgt, lat, gla_norm_w[l])
        osb = _sb_prompt(qs, kstb, vstb, sb_bias[l])
        x3 = _mix_xattn(x1.reshape(b_p, t_p, d), og.reshape(b_p, t_p, nv), osb.reshape(b_p, t_p, hd_s),
                        wo, xattn_norm_w[l], wcq, mk_p, mv_p, wco)
        xp = _ffn(x3.reshape(b_p * t_p, d), ffn2_norm_w[l], *f2, final_norm_w, last)
        seq_major = lambda a: a.reshape(b_p, SB_HEADS, SB_HD, t_p).transpose(0, 3, 1, 2)
        outs[0].append(seq_major(kst))
        outs[1].append(seq_major(vst))
        outs[2].append(s_p.reshape(b_p, GLA_HEADS, GLA_DK, GLA_DV))
        outs[3].append(mk_p.reshape(b_p, -1, X_HEADS, d // X_HEADS))
        outs[4].append(mv_p.reshape(b_p, -1, X_HEADS, d // X_HEADS))

        x1 = _ffn(xs, ffn1_norm_w[l], *f1, final_norm_w, False)
        vg, rg, qs, ks, vs, qgt, kgt, lat = _inproj_sample(x1, mix_norm_w[l], wproj_s)
        tok_major = lambda a: a.reshape(a.shape[0], b_d, t_d).transpose(1, 0, 2)
        feat_major = lambda c: c.transpose(0, 2, 3, 1).reshape(n_pool, hd_s, page)
        og, s_s = _gla_sample(tok_major(qgt), tok_major(kgt), tok_major(lat),
                              vg.reshape(b_d, t_d, nv), rg.reshape(b_d, t_d, nv), state_gla[l], gla_norm_w[l])
        osb = _sb_sample(qs.reshape(b_d, t_d, hd_s), ks.reshape(b_d, t_d, hd_s), vs.reshape(b_d, t_d, hd_s),
                         feat_major(cache_sb_k[l]), feat_major(cache_sb_v[l]), page_table, sb_bias[l])
        x3 = _mix_xattn(x1.reshape(b_d, t_d, d), og, osb, wo, xattn_norm_w[l], wcq,
                        cache_mem_k[l], cache_mem_v[l], wco)
        xs = _ffn(x3.reshape(b_d * t_d, d), ffn2_norm_w[l], *f2, final_norm_w, last)
        outs[5].append(ks.reshape(b_d, t_d, SB_HEADS, SB_HD))
        outs[6].append(vs.reshape(b_d, t_d, SB_HEADS, SB_HD))
        outs[7].append(s_s)

    stk = [jnp.stack(o) for o in outs]
    return (xp.reshape(b_p, t_p, d), xs.reshape(b_d, t_d, d),
            stk[0], stk[1], stk[2], stk[3], stk[4], stk[5], stk[6], stk[7])
```

```python
import functools

import jax
import jax.numpy as jnp
from jax import lax
from jax.experimental import pallas as pl
from jax.experimental.pallas import tpu as pltpu

F32 = jnp.float32
BF16 = jnp.bfloat16

NORM_EPS = 1e-6
FFN_RES = 0.5
GLA_HEADS = 4
GLA_DK = 64
GLA_DV = 128
GLA_GATE_RANK = 16
GLA_GATE_TAU = 16.0
GLA_CHUNK = 64
GLA_TILE = 512
SB_HEADS = 8
SB_HD = 64
SB_MASKED_LOGIT = -1e30
SB_AUG = 16
SB_AUG_ONES = 3
SB_STEPS_PER_TRIP = 4
SB_HEADS_PER_STEP = 4
SB_PAGES_PER_STEP = 16
LOG2E = 1.4426950408889634
X_HEADS = 4
LANE = 128
MXU_DIM = 256
VMEM_LIMIT = 56 * 1024 * 1024


def _cparams(sem):
    return pltpu.CompilerParams(dimension_semantics=sem, vmem_limit_bytes=VMEM_LIMIT)


def _rms(x, w):
    return x * lax.rsqrt(jnp.mean(x * x, axis=-1, keepdims=True) + NORM_EPS) * w


def _softplus(z):
    return jnp.maximum(z, 0.0) + jnp.log(1.0 + jnp.exp(-jnp.abs(z)))


def _log_sigmoid(z):
    return -_softplus(-z)


def _sigmoid(z):
    return 1.0 / (1.0 + jnp.exp(-z))


def _dot(a, b):
    return jnp.dot(a, b, preferred_element_type=F32)


def _dot_nt(a, b):
    return lax.dot_general(a, b, (((1,), (1,)), ((), ())), preferred_element_type=F32)


def _split_bf16(x):
    hi = x.astype(BF16)
    lo = (x - hi.astype(F32)).astype(BF16)
    return hi, lo


def _memkv_kernel(mem_ref, nw_ref, wk_ref, wv_ref, k_ref, v_ref):
    mn = _rms(mem_ref[0], nw_ref[...]).astype(BF16)
    k_ref[0] = _dot(mn, wk_ref[...])
    v_ref[0] = _dot(mn, wv_ref[...])


def _memory_kv(mem, nw, wk, wv):
    b, m, d = mem.shape
    full = lambda shape: pl.BlockSpec(shape, lambda i: (0,) * len(shape))
    row = pl.BlockSpec((1, m, d), lambda i: (i, 0, 0))
    return pl.pallas_call(
        _memkv_kernel,
        grid=(b,),
        in_specs=[row, full((1, d)), full((d, d)), full((d, d))],
        out_specs=[row, row],
        out_shape=[jax.ShapeDtypeStruct((b, m, d), F32)] * 2,
        compiler_params=_cparams(("arbitrary",)),
        name="memory_kv",
    )(mem, nw.reshape(1, d), wk.astype(BF16), wv.astype(BF16))


def _ffn_kernel(x_ref, nw_ref, wg_ref, wu_ref, wd_ref, fw_ref, o_ref, *, final_norm):
    x = x_ref[...]
    h = _rms(x, nw_ref[...]).astype(BF16)
    acc = jnp.zeros_like(x)
    for c in range(wg_ref.shape[1] // MXU_DIM):
        sl = slice(c * MXU_DIM, (c + 1) * MXU_DIM)
        g = _dot(h, wg_ref[:, sl])
        u = _dot(h, wu_ref[:, sl])
        a = (g * _sigmoid(g) * u).astype(BF16)
        acc = acc + _dot(a, wd_ref[sl, :])
    y = x + FFN_RES * acc
    if final_norm:
        y = _rms(y, fw_ref[...])
    o_ref[...] = y


def _pad_ff(w, axis):
    ff = w.shape[axis]
    pad = (-ff) % MXU_DIM
    cfg = [(0, 0), (0, 0)]
    cfg[axis] = (0, pad)
    return jnp.pad(w, cfg).astype(BF16)


def _ffn(x, nw, wg, wu, wd, fw, final_norm):
    rows, d = x.shape
    tm = min(rows, 512)
    assert rows % tm == 0
    ffp = wg.shape[1]
    const = lambda shape: pl.BlockSpec(shape, lambda i: (0, 0), pipeline_mode=pl.Buffered(1))
    row = pl.BlockSpec((tm, d), lambda i: (i, 0))
    return pl.pallas_call(
        functools.partial(_ffn_kernel, final_norm=final_norm),
        grid=(rows // tm,),
        in_specs=[row, const((1, d)), const((d, ffp)), const((d, ffp)), const((ffp, d)), const((1, d))],
        out_specs=row,
        out_shape=jax.ShapeDtypeStruct((rows, d), F32),
        compiler_params=_cparams(("arbitrary",)),
        name="ffn_final" if final_norm else "ffn",
    )(x, nw.reshape(1, d), wg, wu, wd, fw.reshape(1, d))


NQK = GLA_HEADS * GLA_DK
NV = GLA_HEADS * GLA_DV
NS = SB_HEADS * SB_HD


def _gate_log_decay(pre):
    return _log_sigmoid(pre) * (1.0 / GLA_GATE_TAU)


def _inproj_prompt_kernel(x_ref, nw_ref, wn_ref, wt_ref, wglrt_ref, wg2_ref, bg2_ref, wg2t_ref, bg2t_ref,
                          qg_ref, vg_ref, rg_ref, la_ref, qs_ref,
                          kgt_ref, lat_ref, kst_ref, vst_ref, kstb_ref, vstb_ref):
    h = _rms(x_ref[...], nw_ref[...]).astype(BF16)
    y = _dot(h, wn_ref[...])
    qg_ref[...] = y[:, 0:NQK] * (GLA_DK ** -0.5)
    vg_ref[...] = y[:, NQK:NQK + NV]
    rg_ref[...] = y[:, NQK + NV:NQK + 2 * NV]
    o = NQK + 2 * NV
    qs_ref[...] = (y[:, o:o + NS] * (SB_HD ** -0.5 * LOG2E)).astype(BF16)
    glr = y[:, o + NS:o + NS + LANE].astype(BF16)
    la_ref[...] = _gate_log_decay(_dot(glr, wg2_ref[...]) + bg2_ref[...])
    yt = _dot_nt(wt_ref[...], h)
    kgt_ref[...] = yt[0:NQK]
    kst = yt[NQK:NQK + NS]
    vst = yt[NQK + NS:NQK + 2 * NS]
    kst_ref[...] = kst
    vst_ref[...] = vst
    aug_row = lax.broadcasted_iota(jnp.int32, (SB_AUG, kst.shape[1]), 0)
    ones_rows = jnp.where(aug_row < SB_AUG_ONES, 1.0, 0.0).astype(BF16)
    for hh in range(SB_HEADS):
        base = hh * (SB_HD + SB_AUG)
        kstb_ref[base:base + SB_HD, :] = kst[hh * SB_HD:(hh + 1) * SB_HD].astype(BF16)
        kstb_ref[base + SB_HD:base + SB_HD + SB_AUG, :] = ones_rows
    vstb_ref[...] = vst.astype(BF16)
    glrt = _dot_nt(wglrt_ref[...], h).astype(BF16)
    lat_ref[...] = _gate_log_decay(_dot(wg2t_ref[...], glrt) + bg2t_ref[...])


def _inproj_sample_kernel(x_ref, nw_ref, wn_ref, wt_ref, wglrt_ref, wg2t_ref, bg2t_ref,
                          vg_ref, rg_ref, qs_ref, ks_ref, vs_ref, qgt_ref, kgt_ref, lat_ref):
    h = _rms(x_ref[...], nw_ref[...]).astype(BF16)
    y = _dot(h, wn_ref[...])
    vg_ref[...] = y[:, 0:NV]
    rg_ref[...] = y[:, NV:2 * NV]
    o = 2 * NV
    qs_ref[...] = (y[:, o:o + NS] * (SB_HD ** -0.5 * LOG2E)).astype(BF16)
    ks_ref[...] = y[:, o + NS:o + 2 * NS]
    vs_ref[...] = y[:, o + 2 * NS:o + 3 * NS]
    yt = _dot_nt(wt_ref[...], h)
    qgt_ref[...] = yt[0:NQK] * (GLA_DK ** -0.5)
    kgt_ref[...] = yt[NQK:2 * NQK]
    glrt = _dot_nt(wglrt_ref[...], h).astype(BF16)
    lat_ref[...] = _gate_log_decay(_dot(wg2t_ref[...], glrt) + bg2t_ref[...])


def _prep_inproj_weights(w_in, w_gate2, b_gate2):
    sizes = (NQK, NQK, NV, GLA_GATE_RANK, NV, NS, NS, NS)
    offs = [0]
    for s in sizes:
        offs.append(offs[-1] + s)
    qg, kg, vg, glr, rg, qs, ks, vs = [w_in[:, offs[i]:offs[i + 1]] for i in range(8)]
    glr_pad = jnp.pad(glr, ((0, 0), (0, LANE - GLA_GATE_RANK)))
    cat = lambda parts: jnp.concatenate(parts, axis=1)
    shared = dict(
        wglrt=glr.T.astype(BF16),
        wg2=jnp.pad(w_gate2, ((0, LANE - GLA_GATE_RANK), (0, 0))).astype(BF16),
        bg2=b_gate2.reshape(1, NQK),
        wg2t=w_gate2.T.astype(BF16),
        bg2t=b_gate2.reshape(NQK, 1))
    prompt = dict(wn=cat([qg, vg, rg, qs, glr_pad]).astype(BF16), wt=cat([kg, ks, vs]).T.astype(BF16), **shared)
    sample = dict(wn=cat([vg, rg, qs, ks, vs]).astype(BF16), wt=cat([qg, kg]).T.astype(BF16), **shared)
    return prompt, sample


def _inproj_call(kernel_fn, name, x, inputs, row_outs, col_outs):
    rows, d = x.shape
    tm = min(rows, 256)
    assert rows % tm == 0
    const = lambda a: pl.BlockSpec(a.shape, lambda i: (0, 0), pipeline_mode=pl.Buffered(1))
    sds = jax.ShapeDtypeStruct
    return pl.pallas_call(
        kernel_fn,
        grid=(rows // tm,),
        in_specs=[pl.BlockSpec((tm, d), lambda i: (i, 0))] + [const(a) for a in inputs],
        out_specs=[pl.BlockSpec((tm, n), lambda i: (i, 0)) for n, _ in row_outs]
                  + [pl.BlockSpec((n, tm), lambda i: (0, i)) for n, _ in col_outs],
        out_shape=[sds((rows, n), dt) for n, dt in row_outs] + [sds((n, rows), dt) for n, dt in col_outs],
        compiler_params=_cparams(("arbitrary",)),
        name=name,
    )(x, *inputs)


def _inproj_prompt(x, nw, w):
    inputs = [nw.reshape(1, -1), w["wn"], w["wt"], w["wglrt"], w["wg2"], w["bg2"], w["wg2t"], w["bg2t"]]
    return _inproj_call(_inproj_prompt_kernel, "in_proj_prompt", x, inputs,
                        [(NQK, F32), (NV, F32), (NV, F32), (NQK, F32), (NS, BF16)],
                        [(NQK, F32), (NQK, F32), (NS, F32), (NS, F32), (SB_HEADS * (SB_HD + SB_AUG), BF16),
                         (NS, BF16)])


def _inproj_sample(x, nw, w):
    inputs = [nw.reshape(1, -1), w["wn"], w["wt"], w["wglrt"], w["wg2t"], w["bg2t"]]
    return _inproj_call(_inproj_sample_kernel, "in_proj_sample", x, inputs,
                        [(NV, F32), (NV, F32), (NS, BF16), (NS, F32), (NS, F32)],
                        [(NQK, F32), (NQK, F32), (NQK, F32)])


def _gla_out(o, gnw, r):
    o = o * lax.rsqrt(jnp.mean(o * o, axis=-1, keepdims=True) + NORM_EPS) * gnw
    return o * (r * _sigmoid(r))


def _gla_prompt_kernel(q_ref, v_ref, r_ref, la_ref, kt_ref, lat_ref, gnw_ref, o_ref, s_out_ref, s_ref):
    i = pl.program_id(0)
    c = GLA_CHUNK

    @pl.when(i == 0)
    def _():
        s_ref[...] = jnp.zeros_like(s_ref)

    rr = lax.broadcasted_iota(jnp.int32, (c, c), 0)
    cc = lax.broadcasted_iota(jnp.int32, (c, c), 1)
    causal = cc <= rr
    tri = jnp.where(causal, 1.0, 0.0).astype(BF16)
    trit = jnp.where(rr <= cc, 1.0, 0.0).astype(BF16)
    mid = c // 2
    state = [s_ref[h] for h in range(GLA_HEADS)]

    for ch in range(GLA_TILE // c):
        rows = slice(ch * c, (ch + 1) * c)
        la_hi, la_lo = _split_bf16(la_ref[rows, :])
        b = _dot(tri, la_hi) + _dot(tri, la_lo)
        lat_hi, lat_lo = _split_bf16(lat_ref[:, rows])
        bt = _dot(lat_hi, trit) + _dot(lat_lo, trit)
        q = q_ref[rows, :]
        kt = kt_ref[:, rows]
        qe = q * jnp.exp(b)
        qm = q * jnp.exp(b - b[mid:mid + 1, :])
        ktm = kt * jnp.exp(bt[:, mid:mid + 1] - bt)
        ktd = kt * jnp.exp(bt[:, c - 1:c] - bt)
        dec = jnp.exp(bt[:, c - 1:c])
        for h in range(GLA_HEADS):
            ks = slice(h * GLA_DK, (h + 1) * GLA_DK)
            vs = slice(h * GLA_DV, (h + 1) * GLA_DV)
            s = state[h]
            v = v_ref[rows, vs]
            att = jnp.where(causal, _dot(qm[:, ks], ktm[ks, :]), 0.0)
            o = _dot(qe[:, ks], s) + _dot(att, v)
            state[h] = dec[ks, :] * s + _dot(ktd[ks, :], v)
            o_ref[rows, vs] = _gla_out(o, gnw_ref[h:h + 1, :], r_ref[rows, vs])

    for h in range(GLA_HEADS):
        s_ref[h] = state[h]

    @pl.when(i == pl.num_programs(0) - 1)
    def _():
        s_out_ref[...] = s_ref[...]


def _gla_prompt(qg, vg, rg, la, kgt, lat, gnw):
    t = qg.shape[0]
    nqk = GLA_HEADS * GLA_DK
    nv = GLA_HEADS * GLA_DV
    tg = GLA_TILE
    assert t % tg == 0
    row = lambda n: pl.BlockSpec((tg, n), lambda i: (i, 0))
    col = lambda n: pl.BlockSpec((n, tg), lambda i: (0, i))
    return pl.pallas_call(
        _gla_prompt_kernel,
        grid=(t // tg,),
        in_specs=[row(nqk), row(nv), row(nv), row(nqk), col(nqk), col(nqk),
                  pl.BlockSpec((GLA_HEADS, GLA_DV), lambda i: (0, 0))],
        out_specs=[row(nv), pl.BlockSpec((GLA_HEADS, GLA_DK, GLA_DV), lambda i: (0, 0, 0))],
        out_shape=[jax.ShapeDtypeStruct((t, nv), F32),
                   jax.ShapeDtypeStruct((GLA_HEADS, GLA_DK, GLA_DV), F32)],
        scratch_shapes=[pltpu.VMEM((GLA_HEADS, GLA_DK, GLA_DV), F32)],
        compiler_params=_cparams(("arbitrary",)),
        name="gla_prompt",
    )(qg, vg, rg, la, kgt, lat, gnw)


def _gla_sample_kernel(qt_ref, kt_ref, lat_ref, v_ref, r_ref, s0_ref, gnw_ref, o_ref, s_out_ref):
    n_t = v_ref.shape[1]
    for h in range(GLA_HEADS):
        ks = slice(h * GLA_DK, (h + 1) * GLA_DK)
        vs = slice(h * GLA_DV, (h + 1) * GLA_DV)
        s = s0_ref[0, h]
        for t in range(n_t):
            a = jnp.exp(lat_ref[0, ks, t:t + 1])
            s = a * s + kt_ref[0, ks, t:t + 1] * v_ref[0, t:t + 1, vs]
            o = jnp.sum(qt_ref[0, ks, t:t + 1] * s, axis=0, keepdims=True)
            o_ref[0, t:t + 1, vs] = _gla_out(o, gnw_ref[h:h + 1, :], r_ref[0, t:t + 1, vs])
        s_out_ref[0, h] = s


def _gla_sample(qgt, kgt, lat, vg, rg, s0, gnw):
    b, n_t, nv = vg.shape
    nqk = GLA_HEADS * GLA_DK
    colb = pl.BlockSpec((1, nqk, n_t), lambda i: (i, 0, 0))
    rowb = pl.BlockSpec((1, n_t, nv), lambda i: (i, 0, 0))
    st = pl.BlockSpec((1, GLA_HEADS, GLA_DK, GLA_DV), lambda i: (i, 0, 0, 0))
    return pl.pallas_call(
        _gla_sample_kernel,
        grid=(b,),
        in_specs=[colb, colb, colb, rowb, rowb, st, pl.BlockSpec((GLA_HEADS, GLA_DV), lambda i: (0, 0))],
        out_specs=[rowb, st],
        out_shape=[jax.ShapeDtypeStruct((b, n_t, nv), F32),
                   jax.ShapeDtypeStruct((b, GLA_HEADS, GLA_DK, GLA_DV), F32)],
        compiler_params=_cparams(("arbitrary",)),
        name="gla_sample",
    )(qgt, kgt, lat, vg, rg, s0, gnw)


def _neg_tri(n, dtype):
    rr = lax.broadcasted_iota(jnp.int32, (n, n), 0)
    cc = lax.broadcasted_iota(jnp.int32, (n, n), 1)
    return jnp.where(rr >= cc, -1.0, 0.0).astype(dtype)


def _softplus2(z2):
    sign_bit = jnp.uint32(0x80000000)
    neg_abs = pltpu.bitcast(pltpu.bitcast(z2, jnp.uint32) | sign_bit, F32)
    return jnp.maximum(z2, 0.0) + jnp.log2(1.0 + jnp.exp2(neg_abs))


def _sb_prompt_kernel(bias_ref, q_ref, kt_ref, vt_ref, o_ref, q_sc, z_sc, sp_sc, w_sc, scale_sc, *, blk, hp):
    g = pl.program_id(0)
    i = pl.program_id(1)
    ntri = _neg_tri(blk, BF16)
    aug_col = lax.broadcasted_iota(jnp.int32, (blk, SB_AUG), 1)
    for hh in range(hp):
        b2 = jnp.full((blk, SB_AUG), bias_ref[g * hp + hh] * LOG2E, F32)
        b_hi = b2.astype(BF16).astype(F32)
        bias_cols = jnp.where(aug_col == 0, b_hi, jnp.where(aug_col == 1, b2 - b_hi, 0.0))
        for masked in range(2):
            q_sc[masked, hh, :, 0:SB_HD] = q_ref[:, hh * SB_HD:(hh + 1) * SB_HD]
            mask_val = SB_MASKED_LOGIT if masked else 0.0
            q_sc[masked, hh, :, SB_HD:SB_HD + SB_AUG] = jnp.where(aug_col == 2, mask_val, bias_cols).astype(BF16)

    def key_block(ref, hh, j):
        return ref[hh, :, pl.ds(pl.multiple_of(j * blk, blk), blk)]

    def logits(hh, j, masked):
        z = _dot(q_sc[masked, hh], key_block(kt_ref, hh, j))
        return z, _softplus2(z)

    rr = lax.broadcasted_iota(jnp.int32, (blk, blk), 0)
    cc = lax.broadcasted_iota(jnp.int32, (blk, blk), 1)
    mask = cc < rr
    for hh in range(hp):
        z, sp = logits(hh, i, 0)
        z_sc[0, hh] = jnp.where(mask, z, SB_MASKED_LOGIT)
        sp_sc[0, hh] = jnp.where(mask, sp, 0.0).astype(BF16)
        w_sc[0, hh] = jnp.zeros((blk, blk), BF16)
        scale_sc[0, hh] = jnp.ones((1, blk), F32)

    def step(n, slot, st):
        prev = 1 - slot
        j_logits = jnp.maximum(i - n, 0)
        j_value = jnp.clip(i - n + 2, 0, i)
        masked = jnp.where(n <= i, 0, 1)
        out = []
        for hh in range(hp):
            carry, acc = st[2 * hh], st[2 * hh + 1]
            acc = acc + _dot_nt(key_block(vt_ref, hh, j_value), w_sc[prev, hh]) * scale_sc[prev, hh]
            lsum = _dot(sp_sc[prev, hh], ntri)
            w_sc[slot, hh] = jnp.exp2(z_sc[prev, hh] + lsum).astype(BF16)
            scale_sc[slot, hh] = jnp.exp2(carry)
            carry = carry + lsum[:, 0:LANE].T[0:1, :]
            z, sp = logits(hh, j_logits, masked)
            z_sc[slot, hh] = z
            sp_sc[slot, hh] = sp.astype(BF16)
            out.extend((carry, acc))
        return tuple(out)

    spt = SB_STEPS_PER_TRIP

    def body(p, st):
        for u in range(spt):
            st = step(spt * p + 1 + u, (1 + u) % 2, st)
        return st

    init = (jnp.zeros((1, blk), F32), jnp.zeros((SB_HD, blk), F32)) * hp
    state = lax.fori_loop(0, (i + 2 + spt - 1) // spt, body, init)
    for hh in range(hp):
        o_ref[:, hh * SB_HD:(hh + 1) * SB_HD] = state[2 * hh + 1].T


def _sb_prompt(qs, kst, vst, bias):
    t = qs.shape[0]
    blk = min(t, MXU_DIM)
    assert t % blk == 0
    hp = SB_HEADS_PER_STEP
    dk = SB_HD + SB_AUG
    whole_seq = lambda rows_per_head: pl.BlockSpec((hp, rows_per_head, t), lambda h, i: (h, 0, 0),
                                                   pipeline_mode=pl.Buffered(1))
    rows = pl.BlockSpec((blk, hp * SB_HD), lambda h, i: (i, h))
    return pl.pallas_call(
        functools.partial(_sb_prompt_kernel, blk=blk, hp=hp),
        grid=(SB_HEADS // hp, t // blk),
        in_specs=[pl.BlockSpec(memory_space=pltpu.SMEM), rows, whole_seq(dk), whole_seq(SB_HD)],
        out_specs=rows,
        out_shape=jax.ShapeDtypeStruct((t, SB_HEADS * SB_HD), F32),
        scratch_shapes=[pltpu.VMEM((2, hp, blk, dk), BF16), pltpu.VMEM((2, hp, blk, blk), F32),
                        pltpu.VMEM((2, hp, blk, blk), BF16), pltpu.VMEM((2, hp, blk, blk), BF16),
                        pltpu.VMEM((2, hp, 1, blk), F32)],
        compiler_params=_cparams(("arbitrary", "arbitrary")),
        name="sb_prompt",
    )(bias, qs, kst.reshape(SB_HEADS, dk, t), vst.reshape(SB_HEADS, SB_HD, t))


def _sb_sample_kernel(pt_ref, q_ref, bias_ref, kn_ref, vn_ref, *rest, n_t, n_pg):
    k_refs = rest[:n_pg]
    v_refs = rest[n_pg:2 * n_pg]
    o_ref, carry_ref, acc_ref = rest[2 * n_pg:]
    p = pl.program_id(1)
    q = q_ref[0]
    nr, hd = q.shape
    bias = bias_ref[...]
    page = k_refs[0].shape[2]

    @pl.when(p == 0)
    def _():
        t_row = lax.broadcasted_iota(jnp.int32, (nr, 1), 0) // SB_HEADS
        carry = jnp.zeros((nr, 1), F32)
        acc = jnp.zeros((nr, hd), F32)
        for s in range(n_t - 2, -1, -1):
            z = jnp.sum(q * kn_ref[0, s:s + 1, :], axis=-1, keepdims=True) + bias
            m = s < t_row
            carry = carry - jnp.where(m, _softplus2(z), 0.0)
            w = jnp.where(m, jnp.exp2(z + carry), 0.0)
            acc = acc + w * vn_ref[0, s:s + 1, :]
        carry_ref[...] = carry
        acc_ref[...] = acc

    qb = q.astype(BF16)
    ntri = _neg_tri(page, BF16)
    zs = [_dot(qb, k_refs[g][0].astype(BF16)) + bias for g in range(n_pg)]
    ls = [_dot(_softplus2(z).astype(BF16), ntri) for z in zs]
    carry = carry_ref[...]
    acc = jnp.zeros((nr, hd), F32)
    for g in range(n_pg):
        w = jnp.exp2(zs[g] + ls[g] + carry)
        acc = acc + _dot_nt(w.astype(BF16), v_refs[g][0].astype(BF16))
        carry = carry + ls[g][:, 0:1]
    carry_ref[...] = carry
    acc_ref[...] += acc

    @pl.when(p == pl.num_programs(1) - 1)
    def _():
        row_head = lax.broadcasted_iota(jnp.int32, (SB_HEADS, hd), 0)
        lane_head = lax.broadcasted_iota(jnp.int32, (SB_HEADS, hd), 1) // SB_HD
        own = (row_head == lane_head)[None]
        a3 = acc_ref[...].reshape(n_t, SB_HEADS, hd)
        o_ref[0] = jnp.sum(jnp.where(own, a3, 0.0), axis=1)


def _sb_sample(qs, ks_new, vs_new, pool_kt, pool_vt, page_table, bias):
    b, n_t, hd = qs.shape
    n_pages = page_table.shape[1]
    page = pool_kt.shape[2]
    n_pg = SB_PAGES_PER_STEP
    while n_pages % n_pg:
        n_pg //= 2
    nr = SB_HEADS * n_t
    head_of_lane = jnp.arange(hd, dtype=jnp.int32) // SB_HD
    head_of_row = jnp.arange(nr, dtype=jnp.int32) % SB_HEADS
    sel = (head_of_row[:, None] == head_of_lane[None, :])
    q_rows = jnp.repeat(qs.astype(F32), SB_HEADS, axis=1)
    q_bd = jnp.where(sel[None], q_rows, 0.0)
    bias_col = (bias[head_of_row] * LOG2E).reshape(nr, 1)

    def page_spec(g):
        return pl.BlockSpec((1, hd, page), lambda i, p, pt: (pt[i, n_pages - 1 - (p * n_pg + g)], 0, 0))

    grid_spec = pltpu.PrefetchScalarGridSpec(
        num_scalar_prefetch=1,
        grid=(b, n_pages // n_pg),
        in_specs=[pl.BlockSpec((1, nr, hd), lambda i, p, pt: (i, 0, 0)),
                  pl.BlockSpec((nr, 1), lambda i, p, pt: (0, 0)),
                  pl.BlockSpec((1, n_t, hd), lambda i, p, pt: (i, 0, 0)),
                  pl.BlockSpec((1, n_t, hd), lambda i, p, pt: (i, 0, 0))]
                 + [page_spec(g) for g in range(n_pg)] * 2,
        out_specs=pl.BlockSpec((1, n_t, hd), lambda i, p, pt: (i, 0, 0)),
        scratch_shapes=[pltpu.VMEM((nr, 1), F32), pltpu.VMEM((nr, hd), F32)],
    )
    return pl.pallas_call(
        functools.partial(_sb_sample_kernel, n_t=n_t, n_pg=n_pg),
        grid_spec=grid_spec,
        out_shape=jax.ShapeDtypeStruct((b, n_t, hd), F32),
        compiler_params=_cparams(("arbitrary", "arbitrary")),
        name="sb_sample",
    )(page_table, q_bd, bias_col, ks_new, vs_new, *([pool_kt] * n_pg), *([pool_vt] * n_pg))


def _mix_xattn_kernel(x_ref, og_ref, os_ref, wo_ref, nw_ref, wcq_ref, mk_ref, mv_ref, wco_ref, o_ref):
    ng = og_ref.shape[2]
    x = x_ref[0]
    x = x + _dot(og_ref[0].astype(BF16), wo_ref[0:ng, :]) + _dot(os_ref[0].astype(BF16), wo_ref[ng:, :])
    h = _rms(x, nw_ref[...]).astype(BF16)
    d = x.shape[1]
    hd = d // X_HEADS
    q = _dot(h, wcq_ref[...]) * (hd ** -0.5)
    outs = []
    for hh in range(X_HEADS):
        sl = slice(hh * hd, (hh + 1) * hd)
        mk = mk_ref[0, :, sl] if len(mk_ref.shape) == 3 else mk_ref[0, :, hh, :]
        mv = mv_ref[0, :, sl] if len(mv_ref.shape) == 3 else mv_ref[0, :, hh, :]
        s = _dot_nt(q[:, sl], mk)
        s = s - jnp.max(s, axis=-1, keepdims=True)
        e = jnp.exp(s)
        p = e / jnp.sum(e, axis=-1, keepdims=True)
        outs.append(_dot(p, mv))
    o = jnp.concatenate(outs, axis=-1).astype(BF16)
    o_ref[0] = x + _dot(o, wco_ref[...])


def _mix_xattn(x, og, osb, wo, nw, wcq, mk, mv, wco):
    b, r, d = x.shape
    tm = min(r, 512)
    assert r % tm == 0
    ng = og.shape[2]
    ns = osb.shape[2]
    const = lambda shape: pl.BlockSpec(shape, lambda i, j: (0, 0), pipeline_mode=pl.Buffered(1))
    row = lambda n: pl.BlockSpec((1, tm, n), lambda i, j: (i, j, 0))
    mem = pl.BlockSpec((1,) + mk.shape[1:], lambda i, j: (i,) + (0,) * (mk.ndim - 1))
    return pl.pallas_call(
        _mix_xattn_kernel,
        grid=(b, r // tm),
        in_specs=[row(d), row(ng), row(ns), const((ng + ns, d)), const((1, d)), const((d, d)), mem, mem,
                  const((d, d))],
        out_specs=row(d),
        out_shape=jax.ShapeDtypeStruct((b, r, d), F32),
        compiler_params=_cparams(("arbitrary", "arbitrary")),
        name="mix_xattn",
    )(x, og, osb, wo, nw.reshape(1, d), wcq, mk, mv, wco)


def kernel(x_prompt, x_sample, mem_prompt, cache_sb_k, cache_sb_v, page_table, state_gla, cache_mem_k, cache_mem_v, ffn1_norm_w, ffn1_w_gate, ffn1_w_up, ffn1_w_down, mix_norm_w, w_in, w_gate2, b_gate2, gla_norm_w, sb_bias, w_out, xattn_norm_w, mem_norm_w, w_mk, w_mv, w_cq, w_co, ffn2_norm_w, ffn2_w_gate, ffn2_w_up, ffn2_w_down, final_norm_w):
    depth = w_in.shape[0]
    b_p, t_p, d = x_prompt.shape
    b_d, t_d, _ = x_sample.shape
    assert b_p == 1
    n_pool, page = cache_sb_k.shape[1], cache_sb_k.shape[2]
    hd_s = SB_HEADS * SB_HD
    nv = GLA_HEADS * GLA_DV

    xp = x_prompt.reshape(b_p * t_p, d)
    xs = x_sample.reshape(b_d * t_d, d)
    outs = [[] for _ in range(8)]
    for l in range(depth):
        last = l == depth - 1
        f1 = (_pad_ff(ffn1_w_gate[l], 1), _pad_ff(ffn1_w_up[l], 1), _pad_ff(ffn1_w_down[l], 0))
        f2 = (_pad_ff(ffn2_w_gate[l], 1), _pad_ff(ffn2_w_up[l], 1), _pad_ff(ffn2_w_down[l], 0))
        wproj_p, wproj_s = _prep_inproj_weights(w_in[l], w_gate2[l], b_gate2[l])
        wo = w_out[l].astype(BF16)
        wcq = w_cq[l].astype(BF16)
        wco = w_co[l].astype(BF16)

        mk_p, mv_p = _memory_kv(mem_prompt, mem_norm_w[l], w_mk[l], w_mv[l])

        x1 = _ffn(xp, ffn1_norm_w[l], *f1, final_norm_w, False)
        qg, vg, rg, la, qs, kgt, lat, kst, vst, kstb, vstb = _inproj_prompt(x1, mix_norm_w[l], wproj_p)
        og, s_p = _gla_prompt(qg, vg, rg, la, kgt, lat, gla_norm_w[l])
        osb = _sb_prompt(qs, kstb, vstb, sb_bias[l])
        x3 = _mix_xattn(x1.reshape(b_p, t_p, d), og.reshape(b_p, t_p, nv), osb.reshape(b_p, t_p, hd_s),
                        wo, xattn_norm_w[l], wcq, mk_p, mv_p, wco)
        xp = _ffn(x3.reshape(b_p * t_p, d), ffn2_norm_w[l], *f2, final_norm_w, last)
        seq_major = lambda a: a.reshape(b_p, SB_HEADS, SB_HD, t_p).transpose(0, 3, 1, 2)
        outs[0].append(seq_major(kst))
        outs[1].append(seq_major(vst))
        outs[2].append(s_p.reshape(b_p, GLA_HEADS, GLA_DK, GLA_DV))
        outs[3].append(mk_p.reshape(b_p, -1, X_HEADS, d // X_HEADS))
        outs[4].append(mv_p.reshape(b_p, -1, X_HEADS, d // X_HEADS))

        x1 = _ffn(xs, ffn1_norm_w[l], *f1, final_norm_w, False)
        vg, rg, qs, ks, vs, qgt, kgt, lat = _inproj_sample(x1, mix_norm_w[l], wproj_s)
        tok_major = lambda a: a.reshape(a.shape[0], b_d, t_d).transpose(1, 0, 2)
        feat_major = lambda c: c.transpose(0, 2, 3, 1).reshape(n_pool, hd_s, page)
        og, s_s = _gla_sample(tok_major(qgt), tok_major(kgt), tok_major(lat),
                              vg.reshape(b_d, t_d, nv), rg.reshape(b_d, t_d, nv), state_gla[l], gla_norm_w[l])
        osb = _sb_sample(qs.reshape(b_d, t_d, hd_s), ks.reshape(b_d, t_d, hd_s), vs.reshape(b_d, t_d, hd_s),
                         feat_major(cache_sb_k[l]), feat_major(cache_sb_v[l]), page_table, sb_bias[l])
        x3 = _mix_xattn(x1.reshape(b_d, t_d, d), og, osb, wo, xattn_norm_w[l], wcq,
                        cache_mem_k[l], cache_mem_v[l], wco)
        xs = _ffn(x3.reshape(b_d * t_d, d), ffn2_norm_w[l], *f2, final_norm_w, last)
        outs[5].append(ks.reshape(b_d, t_d, SB_HEADS, SB_HD))
        outs[6].append(vs.reshape(b_d, t_d, SB_HEADS, SB_HD))
        outs[7].append(s_s)

    stk = [jnp.stack(o) for o in outs]
    return (xp.reshape(b_p, t_p, d), xs.reshape(b_d, t_d, d),
            stk[0], stk[1], stk[2], stk[3], stk[4], stk[5], stk[6], stk[7])
```

```python
import functools

import jax
import jax.numpy as jnp
from jax import lax
from jax.experimental import pallas as pl
from jax.experimental.pallas import tpu as pltpu

F32 = jnp.float32
BF16 = jnp.bfloat16

NORM_EPS = 1e-6
FFN_RES = 0.5
GLA_HEADS = 4
GLA_DK = 64
GLA_DV = 128
GLA_GATE_RANK = 16
GLA_GATE_TAU = 16.0
GLA_CHUNK = 64
GLA_TILE = 512
SB_HEADS = 8
SB_HD = 64
SB_MASKED_LOGIT = -1e30
SB_AUG = 16
SB_AUG_ONES = 3
SB_QUERY_ROWS = 256
SB_STEPS_PER_TRIP = 4
SB_HEADS_PER_STEP = 4
SB_PAGES_PER_STEP = 16
LOG2E = 1.4426950408889634
X_HEADS = 4
LANE = 128
MXU_DIM = 256
VMEM_LIMIT = 56 * 1024 * 1024


def _cparams(sem):
    return pltpu.CompilerParams(dimension_semantics=sem, vmem_limit_bytes=VMEM_LIMIT)


def _rms(x, w):
    return x * lax.rsqrt(jnp.mean(x * x, axis=-1, keepdims=True) + NORM_EPS) * w


def _softplus(z):
    return jnp.maximum(z, 0.0) + jnp.log(1.0 + jnp.exp(-jnp.abs(z)))


def _log_sigmoid(z):
    return -_softplus(-z)


def _sigmoid(z):
    return 1.0 / (1.0 + jnp.exp(-z))


def _dot(a, b):
    return jnp.dot(a, b, preferred_element_type=F32)


def _dot_nt(a, b):
    return lax.dot_general(a, b, (((1,), (1,)), ((), ())), preferred_element_type=F32)


def _split_bf16(x):
    hi = x.astype(BF16)
    lo = (x - hi.astype(F32)).astype(BF16)
    return hi, lo


def _memkv_kernel(mem_ref, nw_ref, wk_ref, wv_ref, k_ref, v_ref):
    mn = _rms(mem_ref[0], nw_ref[...]).astype(BF16)
    k_ref[0] = _dot(mn, wk_ref[...])
    v_ref[0] = _dot(mn, wv_ref[...])


def _memory_kv(mem, nw, wk, wv):
    b, m, d = mem.shape
    full = lambda shape: pl.BlockSpec(shape, lambda i: (0,) * len(shape))
    row = pl.BlockSpec((1, m, d), lambda i: (i, 0, 0))
    return pl.pallas_call(
        _memkv_kernel,
        grid=(b,),
        in_specs=[row, full((1, d)), full((d, d)), full((d, d))],
        out_specs=[row, row],
        out_shape=[jax.ShapeDtypeStruct((b, m, d), F32)] * 2,
        compiler_params=_cparams(("arbitrary",)),
        name="memory_kv",
    )(mem, nw.reshape(1, d), wk.astype(BF16), wv.astype(BF16))


def _ffn_kernel(x_ref, nw_ref, wg_ref, wu_ref, wd_ref, fw_ref, o_ref, *, final_norm):
    x = x_ref[...]
    h = _rms(x, nw_ref[...]).astype(BF16)
    acc = jnp.zeros_like(x)
    ff = wg_ref.shape[0]
    for c0 in range(0, ff, MXU_DIM):
        sl = slice(c0, min(c0 + MXU_DIM, ff))
        g = _dot_nt(h, wg_ref[sl, :])
        u = _dot_nt(h, wu_ref[sl, :])
        a = (g * _sigmoid(g) * u).astype(BF16)
        acc = acc + _dot(a, wd_ref[sl, :])
    y = x + FFN_RES * acc
    if final_norm:
        y = _rms(y, fw_ref[...])
    o_ref[...] = y


def _ffn_weights(w_gate, w_up, w_down):
    return w_gate.T.astype(BF16), w_up.T.astype(BF16), w_down.astype(BF16)


def _ffn(x, nw, wg, wu, wd, fw, final_norm):
    rows, d = x.shape
    tm = min(rows, 512)
    assert rows % tm == 0
    ff = wg.shape[0]
    const = lambda shape: pl.BlockSpec(shape, lambda i: (0, 0), pipeline_mode=pl.Buffered(1))
    row = pl.BlockSpec((tm, d), lambda i: (i, 0))
    return pl.pallas_call(
        functools.partial(_ffn_kernel, final_norm=final_norm),
        grid=(rows // tm,),
        in_specs=[row, const((1, d)), const((ff, d)), const((ff, d)), const((ff, d)), const((1, d))],
        out_specs=row,
        out_shape=jax.ShapeDtypeStruct((rows, d), F32),
        compiler_params=_cparams(("arbitrary",)),
        name="ffn_final" if final_norm else "ffn",
    )(x, nw.reshape(1, d), wg, wu, wd, fw.reshape(1, d))


NQK = GLA_HEADS * GLA_DK
NV = GLA_HEADS * GLA_DV
NS = SB_HEADS * SB_HD


def _gate_log_decay(pre):
    return _log_sigmoid(pre) * (1.0 / GLA_GATE_TAU)


def _inproj_prompt_kernel(x_ref, nw_ref, wn_ref, wt_ref, wglrt_ref, wg2_ref, bg2_ref, wg2t_ref, bg2t_ref,
                          qg_ref, vg_ref, rg_ref, la_ref, qs_ref,
                          kgt_ref, lat_ref, kst_ref, vst_ref, kstb_ref, vstb_ref):
    h = _rms(x_ref[...], nw_ref[...]).astype(BF16)
    y = _dot(h, wn_ref[...])
    qg_ref[...] = y[:, 0:NQK] * (GLA_DK ** -0.5)
    vg_ref[...] = y[:, NQK:NQK + NV]
    rg_ref[...] = y[:, NQK + NV:NQK + 2 * NV]
    o = NQK + 2 * NV
    qs_ref[...] = (y[:, o:o + NS] * (SB_HD ** -0.5 * LOG2E)).astype(BF16)
    glr = y[:, o + NS:o + NS + LANE].astype(BF16)
    la_ref[...] = _gate_log_decay(_dot(glr, wg2_ref[...]) + bg2_ref[...])
    yt = _dot_nt(wt_ref[...], h)
    kgt_ref[...] = yt[0:NQK]
    kst = yt[NQK:NQK + NS]
    vst = yt[NQK + NS:NQK + 2 * NS]
    kst_ref[...] = kst
    vst_ref[...] = vst
    aug_row = lax.broadcasted_iota(jnp.int32, (SB_AUG, kst.shape[1]), 0)
    ones_rows = jnp.where(aug_row < SB_AUG_ONES, 1.0, 0.0).astype(BF16)
    for hh in range(SB_HEADS):
        base = hh * (SB_HD + SB_AUG)
        kstb_ref[base:base + SB_HD, :] = kst[hh * SB_HD:(hh + 1) * SB_HD].astype(BF16)
        kstb_ref[base + SB_HD:base + SB_HD + SB_AUG, :] = ones_rows
    vstb_ref[...] = vst.astype(BF16)
    glrt = _dot_nt(wglrt_ref[...], h).astype(BF16)
    lat_ref[...] = _gate_log_decay(_dot(wg2t_ref[...], glrt) + bg2t_ref[...])


def _inproj_sample_kernel(x_ref, nw_ref, wn_ref, wt_ref, wglrt_ref, wg2t_ref, bg2t_ref,
                          vg_ref, rg_ref, qs_ref, ks_ref, vs_ref, qgt_ref, kgt_ref, lat_ref):
    h = _rms(x_ref[...], nw_ref[...]).astype(BF16)
    y = _dot(h, wn_ref[...])
    vg_ref[...] = y[:, 0:NV]
    rg_ref[...] = y[:, NV:2 * NV]
    o = 2 * NV
    qs_ref[...] = (y[:, o:o + NS] * (SB_HD ** -0.5 * LOG2E)).astype(BF16)
    ks_ref[...] = y[:, o + NS:o + 2 * NS]
    vs_ref[...] = y[:, o + 2 * NS:o + 3 * NS]
    yt = _dot_nt(wt_ref[...], h)
    qgt_ref[...] = yt[0:NQK] * (GLA_DK ** -0.5)
    kgt_ref[...] = yt[NQK:2 * NQK]
    glrt = _dot_nt(wglrt_ref[...], h).astype(BF16)
    lat_ref[...] = _gate_log_decay(_dot(wg2t_ref[...], glrt) + bg2t_ref[...])


def _prep_inproj_weights(w_in, w_gate2, b_gate2):
    sizes = (NQK, NQK, NV, GLA_GATE_RANK, NV, NS, NS, NS)
    offs = [0]
    for s in sizes:
        offs.append(offs[-1] + s)
    qg, kg, vg, glr, rg, qs, ks, vs = [w_in[:, offs[i]:offs[i + 1]] for i in range(8)]
    glr_pad = jnp.pad(glr, ((0, 0), (0, LANE - GLA_GATE_RANK)))
    cat = lambda parts: jnp.concatenate(parts, axis=1)
    shared = dict(
        wglrt=glr.T.astype(BF16),
        wg2=jnp.pad(w_gate2, ((0, LANE - GLA_GATE_RANK), (0, 0))).astype(BF16),
        bg2=b_gate2.reshape(1, NQK),
        wg2t=w_gate2.T.astype(BF16),
        bg2t=b_gate2.reshape(NQK, 1))
    prompt = dict(wn=cat([qg, vg, rg, qs, glr_pad]).astype(BF16), wt=cat([kg, ks, vs]).T.astype(BF16), **shared)
    sample = dict(wn=cat([vg, rg, qs, ks, vs]).astype(BF16), wt=cat([qg, kg]).T.astype(BF16), **shared)
    return prompt, sample


def _inproj_call(kernel_fn, name, x, inputs, row_outs, col_outs):
    rows, d = x.shape
    tm = min(rows, 512)
    assert rows % tm == 0
    const = lambda a: pl.BlockSpec(a.shape, lambda i: (0, 0), pipeline_mode=pl.Buffered(1))
    sds = jax.ShapeDtypeStruct
    return pl.pallas_call(
        kernel_fn,
        grid=(rows // tm,),
        in_specs=[pl.BlockSpec((tm, d), lambda i: (i, 0))] + [const(a) for a in inputs],
        out_specs=[pl.BlockSpec((tm, n), lambda i: (i, 0)) for n, _ in row_outs]
                  + [pl.BlockSpec((n, tm), lambda i: (0, i)) for n, _ in col_outs],
        out_shape=[sds((rows, n), dt) for n, dt in row_outs] + [sds((n, rows), dt) for n, dt in col_outs],
        compiler_params=_cparams(("arbitrary",)),
        name=name,
    )(x, *inputs)


def _inproj_prompt(x, nw, w):
    inputs = [nw.reshape(1, -1), w["wn"], w["wt"], w["wglrt"], w["wg2"], w["bg2"], w["wg2t"], w["bg2t"]]
    return _inproj_call(_inproj_prompt_kernel, "in_proj_prompt", x, inputs,
                        [(NQK, F32), (NV, F32), (NV, F32), (NQK, F32), (NS, BF16)],
                        [(NQK, F32), (NQK, F32), (NS, F32), (NS, F32), (SB_HEADS * (SB_HD + SB_AUG), BF16),
                         (NS, BF16)])


def _inproj_sample(x, nw, w):
    inputs = [nw.reshape(1, -1), w["wn"], w["wt"], w["wglrt"], w["wg2t"], w["bg2t"]]
    return _inproj_call(_inproj_sample_kernel, "in_proj_sample", x, inputs,
                        [(NV, F32), (NV, F32), (NS, BF16), (NS, F32), (NS, F32)],
                        [(NQK, F32), (NQK, F32), (NQK, F32)])


def _gla_out(o, gnw, r):
    o = o * lax.rsqrt(jnp.mean(o * o, axis=-1, keepdims=True) + NORM_EPS) * gnw
    return o * (r * _sigmoid(r))


def _gla_prompt_kernel(q_ref, v_ref, r_ref, la_ref, kt_ref, lat_ref, gnw_ref, o_ref, s_out_ref, s_ref):
    i = pl.program_id(0)
    c = GLA_CHUNK

    @pl.when(i == 0)
    def _():
        s_ref[...] = jnp.zeros_like(s_ref)

    rr = lax.broadcasted_iota(jnp.int32, (c, c), 0)
    cc = lax.broadcasted_iota(jnp.int32, (c, c), 1)
    causal = cc <= rr
    tri = jnp.where(causal, 1.0, 0.0).astype(BF16)
    trit = jnp.where(rr <= cc, 1.0, 0.0).astype(BF16)
    mid = c // 2
    state = [s_ref[h] for h in range(GLA_HEADS)]

    for ch in range(GLA_TILE // c):
        rows = slice(ch * c, (ch + 1) * c)
        la_hi, la_lo = _split_bf16(la_ref[rows, :])
        b = _dot(tri, la_hi) + _dot(tri, la_lo)
        lat_hi, lat_lo = _split_bf16(lat_ref[:, rows])
        bt = _dot(lat_hi, trit) + _dot(lat_lo, trit)
        q = q_ref[rows, :]
        kt = kt_ref[:, rows]
        qe = q * jnp.exp(b)
        qm = q * jnp.exp(b - b[mid:mid + 1, :])
        ktm = kt * jnp.exp(bt[:, mid:mid + 1] - bt)
        ktd = kt * jnp.exp(bt[:, c - 1:c] - bt)
        dec = jnp.exp(bt[:, c - 1:c])
        for h in range(GLA_HEADS):
            ks = slice(h * GLA_DK, (h + 1) * GLA_DK)
            vs = slice(h * GLA_DV, (h + 1) * GLA_DV)
            s = state[h]
            v = v_ref[rows, vs]
            att = jnp.where(causal, _dot(qm[:, ks], ktm[ks, :]), 0.0)
            o = _dot(qe[:, ks], s) + _dot(att, v)
            state[h] = dec[ks, :] * s + _dot(ktd[ks, :], v)
            o_ref[rows, vs] = _gla_out(o, gnw_ref[h:h + 1, :], r_ref[rows, vs])

    for h in range(GLA_HEADS):
        s_ref[h] = state[h]

    @pl.when(i == pl.num_programs(0) - 1)
    def _():
        s_out_ref[...] = s_ref[...]


def _gla_prompt(qg, vg, rg, la, kgt, lat, gnw):
    t = qg.shape[0]
    nqk = GLA_HEADS * GLA_DK
    nv = GLA_HEADS * GLA_DV
    tg = GLA_TILE
    assert t % tg == 0
    row = lambda n: pl.BlockSpec((tg, n), lambda i: (i, 0))
    col = lambda n: pl.BlockSpec((n, tg), lambda i: (0, i))
    return pl.pallas_call(
        _gla_prompt_kernel,
        grid=(t // tg,),
        in_specs=[row(nqk), row(nv), row(nv), row(nqk), col(nqk), col(nqk),
                  pl.BlockSpec((GLA_HEADS, GLA_DV), lambda i: (0, 0))],
        out_specs=[row(nv), pl.BlockSpec((GLA_HEADS, GLA_DK, GLA_DV), lambda i: (0, 0, 0))],
        out_shape=[jax.ShapeDtypeStruct((t, nv), F32),
                   jax.ShapeDtypeStruct((GLA_HEADS, GLA_DK, GLA_DV), F32)],
        scratch_shapes=[pltpu.VMEM((GLA_HEADS, GLA_DK, GLA_DV), F32)],
        compiler_params=_cparams(("arbitrary",)),
        name="gla_prompt",
    )(qg, vg, rg, la, kgt, lat, gnw)


def _gla_sample_kernel(qt_ref, kt_ref, lat_ref, v_ref, r_ref, s0_ref, gnw_ref, o_ref, s_out_ref):
    n_t = v_ref.shape[1]
    for h in range(GLA_HEADS):
        ks = slice(h * GLA_DK, (h + 1) * GLA_DK)
        vs = slice(h * GLA_DV, (h + 1) * GLA_DV)
        s = s0_ref[0, h]
        for t in range(n_t):
            a = jnp.exp(lat_ref[0, ks, t:t + 1])
            s = a * s + kt_ref[0, ks, t:t + 1] * v_ref[0, t:t + 1, vs]
            o = jnp.sum(qt_ref[0, ks, t:t + 1] * s, axis=0, keepdims=True)
            o_ref[0, t:t + 1, vs] = _gla_out(o, gnw_ref[h:h + 1, :], r_ref[0, t:t + 1, vs])
        s_out_ref[0, h] = s


def _gla_sample(qgt, kgt, lat, vg, rg, s0, gnw):
    b, n_t, nv = vg.shape
    nqk = GLA_HEADS * GLA_DK
    colb = pl.BlockSpec((1, nqk, n_t), lambda i: (i, 0, 0))
    rowb = pl.BlockSpec((1, n_t, nv), lambda i: (i, 0, 0))
    st = pl.BlockSpec((1, GLA_HEADS, GLA_DK, GLA_DV), lambda i: (i, 0, 0, 0))
    return pl.pallas_call(
        _gla_sample_kernel,
        grid=(b,),
        in_specs=[colb, colb, colb, rowb, rowb, st, pl.BlockSpec((GLA_HEADS, GLA_DV), lambda i: (0, 0))],
        out_specs=[rowb, st],
        out_shape=[jax.ShapeDtypeStruct((b, n_t, nv), F32),
                   jax.ShapeDtypeStruct((b, GLA_HEADS, GLA_DK, GLA_DV), F32)],
        compiler_params=_cparams(("arbitrary",)),
        name="gla_sample",
    )(qgt, kgt, lat, vg, rg, s0, gnw)


def _neg_tri(n, dtype):
    rr = lax.broadcasted_iota(jnp.int32, (n, n), 0)
    cc = lax.broadcasted_iota(jnp.int32, (n, n), 1)
    return jnp.where(rr >= cc, -1.0, 0.0).astype(dtype)


def _softplus2(z2):
    sign_bit = jnp.uint32(0x80000000)
    neg_abs = pltpu.bitcast(pltpu.bitcast(z2, jnp.uint32) | sign_bit, F32)
    return jnp.maximum(z2, 0.0) + jnp.log2(1.0 + jnp.exp2(neg_abs))


def _sb_prompt_kernel(bias_ref, q_ref, kt_ref, vt_ref, o_ref, q_sc, z_sc, sp_sc, w_sc, scale_sc, *, qr, blk, hp):
    g = pl.program_id(0)
    i = pl.program_id(1)
    n_diag = qr // blk
    top = (i + 1) * n_diag - 1
    ntri = _neg_tri(blk, BF16)
    aug_col = lax.broadcasted_iota(jnp.int32, (qr, SB_AUG), 1)
    for hh in range(hp):
        b2 = jnp.full((qr, SB_AUG), bias_ref[g * hp + hh] * LOG2E, F32)
        b_hi = b2.astype(BF16).astype(F32)
        bias_cols = jnp.where(aug_col == 0, b_hi, jnp.where(aug_col == 1, b2 - b_hi, 0.0))
        for masked in range(2):
            q_sc[masked, hh, :, 0:SB_HD] = q_ref[:, hh * SB_HD:(hh + 1) * SB_HD]
            mask_val = SB_MASKED_LOGIT if masked else 0.0
            q_sc[masked, hh, :, SB_HD:SB_HD + SB_AUG] = jnp.where(aug_col == 2, mask_val, bias_cols).astype(BF16)
        w_sc[0, hh] = jnp.zeros((qr, blk), BF16)
        scale_sc[0, hh] = jnp.ones((1, qr), F32)

    def key_block(ref, hh, j):
        return ref[hh, :, pl.ds(pl.multiple_of(j * blk, blk), blk)]

    rr = lax.broadcasted_iota(jnp.int32, (qr, blk), 0)
    cc = lax.broadcasted_iota(jnp.int32, (qr, blk), 1)

    def logits_stage(n, slot, hh, diagonal):
        j = jnp.maximum(top - n, 0)
        masked = jnp.where(n <= top, 0, 1)
        z = _dot(q_sc[masked, hh], key_block(kt_ref, hh, j))
        sp = _softplus2(z)
        if diagonal:
            keep = cc + (n_diag - 1 - n) * blk < rr
            z = jnp.where(keep, z, SB_MASKED_LOGIT)
            sp = jnp.where(keep, sp, 0.0)
        z_sc[slot, hh] = z
        sp_sc[slot, hh] = sp.astype(BF16)

    def step(n, slot, st, diagonal=False):
        prev = 1 - slot
        j_value = jnp.clip(top - n + 2, 0, top)
        out = []
        for hh in range(hp):
            carry, acc = st[2 * hh], st[2 * hh + 1]
            acc = acc + _dot_nt(key_block(vt_ref, hh, j_value), w_sc[prev, hh]) * scale_sc[prev, hh]
            lsum = _dot(sp_sc[prev, hh], ntri)
            w_sc[slot, hh] = jnp.exp2(z_sc[prev, hh] + lsum).astype(BF16)
            scale_sc[slot, hh] = jnp.exp2(carry)
            carry = carry + lsum[:, 0:LANE].T[0:1, :]
            logits_stage(n, slot, hh, diagonal)
            out.extend((carry, acc))
        return tuple(out)

    state = (jnp.zeros((1, qr), F32), jnp.zeros((SB_HD, qr), F32)) * hp
    for hh in range(hp):
        logits_stage(0, 0, hh, True)
    for n in range(1, n_diag):
        state = step(n, n % 2, state, True)

    spt = SB_STEPS_PER_TRIP
    first = n_diag

    def body(p, st):
        for u in range(spt):
            st = step(spt * p + first + u, (first + u) % 2, st)
        return st

    def pair(n0, st):
        return step(n0 + 1, (first + 1) % 2, step(n0, first % 2, st))

    n_steps = top + 3 - first
    n_trips = n_steps // spt
    state = lax.fori_loop(0, n_trips, body, state)
    for k in range(spt // 2):
        n0 = first + spt * n_trips + 2 * k
        state = lax.cond(n0 <= top + 2, functools.partial(pair, n0), lambda st: st, state)
    for hh in range(hp):
        o_ref[:, hh * SB_HD:(hh + 1) * SB_HD] = state[2 * hh + 1].T


def _sb_prompt(qs, kst, vst, bias):
    t = qs.shape[0]
    blk = min(t, MXU_DIM)
    qr = min(t, SB_QUERY_ROWS)
    assert t % qr == 0 and qr % blk == 0
    hp = SB_HEADS_PER_STEP
    dk = SB_HD + SB_AUG
    whole_seq = lambda rows_per_head: pl.BlockSpec((hp, rows_per_head, t), lambda h, i: (h, 0, 0),
                                                   pipeline_mode=pl.Buffered(1))
    rows = pl.BlockSpec((qr, hp * SB_HD), lambda h, i: (i, h))
    return pl.pallas_call(
        functools.partial(_sb_prompt_kernel, qr=qr, blk=blk, hp=hp),
        grid=(SB_HEADS // hp, t // qr),
        in_specs=[pl.BlockSpec(memory_space=pltpu.SMEM), rows, whole_seq(dk), whole_seq(SB_HD)],
        out_specs=rows,
        out_shape=jax.ShapeDtypeStruct((t, SB_HEADS * SB_HD), F32),
        scratch_shapes=[pltpu.VMEM((2, hp, qr, dk), BF16), pltpu.VMEM((2, hp, qr, blk), F32),
                        pltpu.VMEM((2, hp, qr, blk), BF16), pltpu.VMEM((2, hp, qr, blk), BF16),
                        pltpu.VMEM((2, hp, 1, qr), F32)],
        compiler_params=_cparams(("arbitrary", "arbitrary")),
        name="sb_prompt",
    )(bias, qs, kst.reshape(SB_HEADS, dk, t), vst.reshape(SB_HEADS, SB_HD, t))


def _sb_sample_kernel(pt_ref, q_ref, bias_ref, kn_ref, vn_ref, *rest, n_t, n_pg):
    k_refs = rest[:n_pg]
    v_refs = rest[n_pg:2 * n_pg]
    o_ref, carry_ref, acc_ref = rest[2 * n_pg:]
    p = pl.program_id(1)
    q = q_ref[0]
    nr, hd = q.shape
    bias = bias_ref[...]
    page = k_refs[0].shape[2]

    @pl.when(p == 0)
    def _():
        t_row = lax.broadcasted_iota(jnp.int32, (nr, 1), 0) // SB_HEADS
        carry = jnp.zeros((nr, 1), F32)
        acc = jnp.zeros((nr, hd), F32)
        for s in range(n_t - 2, -1, -1):
            z = jnp.sum(q * kn_ref[0, s:s + 1, :], axis=-1, keepdims=True) + bias
            m = s < t_row
            carry = carry - jnp.where(m, _softplus2(z), 0.0)
            w = jnp.where(m, jnp.exp2(z + carry), 0.0)
            acc = acc + w * vn_ref[0, s:s + 1, :]
        carry_ref[...] = carry
        acc_ref[...] = acc

    qb = q.astype(BF16)
    ntri = _neg_tri(page, BF16)
    zs = [_dot(qb, k_refs[g][0].astype(BF16)) + bias for g in range(n_pg)]
    ls = [_dot(_softplus2(z).astype(BF16), ntri) for z in zs]
    carry = carry_ref[...]
    acc = jnp.zeros((nr, hd), F32)
    for g in range(n_pg):
        w = jnp.exp2(zs[g] + ls[g] + carry)
        acc = acc + _dot_nt(w.astype(BF16), v_refs[g][0].astype(BF16))
        carry = carry + ls[g][:, 0:1]
    carry_ref[...] = carry
    acc_ref[...] += acc

    @pl.when(p == pl.num_programs(1) - 1)
    def _():
        row_head = lax.broadcasted_iota(jnp.int32, (SB_HEADS, hd), 0)
        lane_head = lax.broadcasted_iota(jnp.int32, (SB_HEADS, hd), 1) // SB_HD
        own = (row_head == lane_head)[None]
        a3 = acc_ref[...].reshape(n_t, SB_HEADS, hd)
        o_ref[0] = jnp.sum(jnp.where(own, a3, 0.0), axis=1)


def _sb_sample(qs, ks_new, vs_new, pool_kt, pool_vt, page_table, bias):
    b, n_t, hd = qs.shape
    n_pages = page_table.shape[1]
    page = pool_kt.shape[2]
    n_pg = SB_PAGES_PER_STEP
    while n_pages % n_pg:
        n_pg //= 2
    nr = SB_HEADS * n_t
    head_of_lane = jnp.arange(hd, dtype=jnp.int32) // SB_HD
    head_of_row = jnp.arange(nr, dtype=jnp.int32) % SB_HEADS
    sel = (head_of_row[:, None] == head_of_lane[None, :])
    q_rows = jnp.repeat(qs.astype(F32), SB_HEADS, axis=1)
    q_bd = jnp.where(sel[None], q_rows, 0.0)
    bias_col = (bias[head_of_row] * LOG2E).reshape(nr, 1)

    def page_spec(g):
        return pl.BlockSpec((1, hd, page), lambda i, p, pt: (pt[i, n_pages - 1 - (p * n_pg + g)], 0, 0))

    grid_spec = pltpu.PrefetchScalarGridSpec(
        num_scalar_prefetch=1,
        grid=(b, n_pages // n_pg),
        in_specs=[pl.BlockSpec((1, nr, hd), lambda i, p, pt: (i, 0, 0)),
                  pl.BlockSpec((nr, 1), lambda i, p, pt: (0, 0)),
                  pl.BlockSpec((1, n_t, hd), lambda i, p, pt: (i, 0, 0)),
                  pl.BlockSpec((1, n_t, hd), lambda i, p, pt: (i, 0, 0))]
                 + [page_spec(g) for g in range(n_pg)] * 2,
        out_specs=pl.BlockSpec((1, n_t, hd), lambda i, p, pt: (i, 0, 0)),
        scratch_shapes=[pltpu.VMEM((nr, 1), F32), pltpu.VMEM((nr, hd), F32)],
    )
    return pl.pallas_call(
        functools.partial(_sb_sample_kernel, n_t=n_t, n_pg=n_pg),
        grid_spec=grid_spec,
        out_shape=jax.ShapeDtypeStruct((b, n_t, hd), F32),
        compiler_params=_cparams(("arbitrary", "arbitrary")),
        name="sb_sample",
    )(page_table, q_bd, bias_col, ks_new, vs_new, *([pool_kt] * n_pg), *([pool_vt] * n_pg))


def _mix_residual(x, og, osb, wo_ref):
    ng = og.shape[1]
    return x + _dot(og.astype(BF16), wo_ref[0:ng, :]) + _dot(osb.astype(BF16), wo_ref[ng:, :])


def _xattn_query(x, nw_ref, wcq_ref):
    h = _rms(x, nw_ref[...]).astype(BF16)
    return _dot(h, wcq_ref[...]) * ((x.shape[1] // X_HEADS) ** -0.5)


def _xattn_heads(q, mk_ref, mv_ref):
    hd = q.shape[1] // X_HEADS
    outs = []
    for hh in range(X_HEADS):
        sl = slice(hh * hd, (hh + 1) * hd)
        mk = mk_ref[0, :, sl] if len(mk_ref.shape) == 3 else mk_ref[0, :, hh, :]
        mv = mv_ref[0, :, sl] if len(mv_ref.shape) == 3 else mv_ref[0, :, hh, :]
        s = _dot_nt(q[:, sl], mk)
        s = s - jnp.max(s, axis=-1, keepdims=True)
        e = jnp.exp(s)
        p = e / jnp.sum(e, axis=-1, keepdims=True)
        outs.append(_dot(p, mv))
    return jnp.concatenate(outs, axis=-1)


def _mix_xattn_kernel(x_ref, og_ref, os_ref, wo_ref, nw_ref, wcq_ref, mk_ref, mv_ref, wco_ref, o_ref):
    x = _mix_residual(x_ref[0], og_ref[0], os_ref[0], wo_ref)
    o = _xattn_heads(_xattn_query(x, nw_ref, wcq_ref), mk_ref, mv_ref)
    o_ref[0] = x + _dot(o.astype(BF16), wco_ref[...])


def _mix_query_kernel(x_ref, og_ref, os_ref, wo_ref, nw_ref, wcq_ref, x2_ref, q_ref):
    x = _mix_residual(x_ref[...], og_ref[...], os_ref[...], wo_ref)
    x2_ref[...] = x
    q_ref[...] = _xattn_query(x, nw_ref, wcq_ref)


def _xattn_core_kernel(q_ref, mk_ref, mv_ref, o_ref):
    o_ref[0] = _xattn_heads(q_ref[0], mk_ref, mv_ref)


def _xattn_out_kernel(x_ref, o_ref, wco_ref, y_ref):
    y_ref[...] = x_ref[...] + _dot(o_ref[...].astype(BF16), wco_ref[...])


def _mix_xattn_sample(x, og, osb, wo, nw, wcq, mk, mv, wco):
    b, r, d = x.shape
    rows = b * r
    full = lambda a: pl.BlockSpec(a.shape, lambda *_: (0,) * a.ndim)
    sds = jax.ShapeDtypeStruct
    ins = [x.reshape(rows, d), og.reshape(rows, -1), osb.reshape(rows, -1), wo, nw.reshape(1, d), wcq]
    x2, q = pl.pallas_call(
        _mix_query_kernel, grid=(1,), in_specs=[full(a) for a in ins],
        out_specs=[pl.BlockSpec((rows, d), lambda i: (0, 0))] * 2,
        out_shape=[sds((rows, d), F32)] * 2,
        compiler_params=_cparams(("arbitrary",)), name="mix_query_sample")(*ins)
    per_batch = pl.BlockSpec((1, r, d), lambda i: (i, 0, 0))
    mem = pl.BlockSpec((1,) + mk.shape[1:], lambda i: (i,) + (0,) * (mk.ndim - 1))
    o = pl.pallas_call(
        _xattn_core_kernel, grid=(b,), in_specs=[per_batch, mem, mem], out_specs=per_batch,
        out_shape=sds((b, r, d), F32),
        compiler_params=_cparams(("arbitrary",)), name="xattn_core_sample")(q.reshape(b, r, d), mk, mv)
    ins = [x2, o.reshape(rows, d), wco]
    y = pl.pallas_call(
        _xattn_out_kernel, grid=(1,), in_specs=[full(a) for a in ins],
        out_specs=pl.BlockSpec((rows, d), lambda i: (0, 0)),
        out_shape=sds((rows, d), F32),
        compiler_params=_cparams(("arbitrary",)), name="xattn_out_sample")(*ins)
    return y.reshape(b, r, d)


def _mix_xattn(x, og, osb, wo, nw, wcq, mk, mv, wco):
    b, r, d = x.shape
    tm = min(r, 1024)
    assert r % tm == 0
    ng = og.shape[2]
    ns = osb.shape[2]
    const = lambda shape: pl.BlockSpec(shape, lambda i, j: (0, 0), pipeline_mode=pl.Buffered(1))
    row = lambda n: pl.BlockSpec((1, tm, n), lambda i, j: (i, j, 0))
    mem = pl.BlockSpec((1,) + mk.shape[1:], lambda i, j: (i,) + (0,) * (mk.ndim - 1))
    return pl.pallas_call(
        _mix_xattn_kernel,
        grid=(b, r // tm),
        in_specs=[row(d), row(ng), row(ns), const((ng + ns, d)), const((1, d)), const((d, d)), mem, mem,
                  const((d, d))],
        out_specs=row(d),
        out_shape=jax.ShapeDtypeStruct((b, r, d), F32),
        compiler_params=_cparams(("arbitrary", "arbitrary")),
        name="mix_xattn",
    )(x, og, osb, wo, nw.reshape(1, d), wcq, mk, mv, wco)


def kernel(x_prompt, x_sample, mem_prompt, cache_sb_k, cache_sb_v, page_table, state_gla, cache_mem_k, cache_mem_v, ffn1_norm_w, ffn1_w_gate, ffn1_w_up, ffn1_w_down, mix_norm_w, w_in, w_gate2, b_gate2, gla_norm_w, sb_bias, w_out, xattn_norm_w, mem_norm_w, w_mk, w_mv, w_cq, w_co, ffn2_norm_w, ffn2_w_gate, ffn2_w_up, ffn2_w_down, final_norm_w):
    depth = w_in.shape[0]
    b_p, t_p, d = x_prompt.shape
    b_d, t_d, _ = x_sample.shape
    assert b_p == 1
    n_pool, page = cache_sb_k.shape[1], cache_sb_k.shape[2]
    hd_s = SB_HEADS * SB_HD
    nv = GLA_HEADS * GLA_DV

    xp = x_prompt.reshape(b_p * t_p, d)
    xs = x_sample.reshape(b_d * t_d, d)
    outs = [[] for _ in range(8)]
    for l in range(depth):
        last = l == depth - 1
        f1 = _ffn_weights(ffn1_w_gate[l], ffn1_w_up[l], ffn1_w_down[l])
        f2 = _ffn_weights(ffn2_w_gate[l], ffn2_w_up[l], ffn2_w_down[l])
        wproj_p, wproj_s = _prep_inproj_weights(w_in[l], w_gate2[l], b_gate2[l])
        wo = w_out[l].astype(BF16)
        wcq = w_cq[l].astype(BF16)
        wco = w_co[l].astype(BF16)

        mk_p, mv_p = _memory_kv(mem_prompt, mem_norm_w[l], w_mk[l], w_mv[l])

        x1 = _ffn(xp, ffn1_norm_w[l], *f1, final_norm_w, False)
        qg, vg, rg, la, qs, kgt, lat, kst, vst, kstb, vstb = _inproj_prompt(x1, mix_norm_w[l], wproj_p)
        og, s_p = _gla_prompt(qg, vg, rg, la, kgt, lat, gla_norm_w[l])
        osb = _sb_prompt(qs, kstb, vstb, sb_bias[l])
        x3 = _mix_xattn(x1.reshape(b_p, t_p, d), og.reshape(b_p, t_p, nv), osb.reshape(b_p, t_p, hd_s),
                        wo, xattn_norm_w[l], wcq, mk_p, mv_p, wco)
        xp = _ffn(x3.reshape(b_p * t_p, d), ffn2_norm_w[l], *f2, final_norm_w, last)
        seq_major = lambda a: a.reshape(b_p, SB_HEADS, SB_HD, t_p).transpose(0, 3, 1, 2)
        outs[0].append(seq_major(kst))
        outs[1].append(seq_major(vst))
        outs[2].append(s_p.reshape(b_p, GLA_HEADS, GLA_DK, GLA_DV))
        outs[3].append(mk_p.reshape(b_p, -1, X_HEADS, d // X_HEADS))
        outs[4].append(mv_p.reshape(b_p, -1, X_HEADS, d // X_HEADS))

        x1 = _ffn(xs, ffn1_norm_w[l], *f1, final_norm_w, False)
        vg, rg, qs, ks, vs, qgt, kgt, lat = _inproj_sample(x1, mix_norm_w[l], wproj_s)
        tok_major = lambda a: a.reshape(a.shape[0], b_d, t_d).transpose(1, 0, 2)
        feat_major = lambda c: c.transpose(0, 2, 3, 1).reshape(n_pool, hd_s, page)
        og, s_s = _gla_sample(tok_major(qgt), tok_major(kgt), tok_major(lat),
                              vg.reshape(b_d, t_d, nv), rg.reshape(b_d, t_d, nv), state_gla[l], gla_norm_w[l])
        osb = _sb_sample(qs.reshape(b_d, t_d, hd_s), ks.reshape(b_d, t_d, hd_s), vs.reshape(b_d, t_d, hd_s),
                         feat_major(cache_sb_k[l]), feat_major(cache_sb_v[l]), page_table, sb_bias[l])
        x3 = _mix_xattn_sample(x1.reshape(b_d, t_d, d), og, osb, wo, xattn_norm_w[l], wcq,
                               cache_mem_k[l], cache_mem_v[l], wco)
        xs = _ffn(x3.reshape(b_d * t_d, d), ffn2_norm_w[l], *f2, final_norm_w, last)
        outs[5].append(ks.reshape(b_d, t_d, SB_HEADS, SB_HD))
        outs[6].append(vs.reshape(b_d, t_d, SB_HEADS, SB_HD))
        outs[7].append(s_s)

    stk = [jnp.stack(o) for o in outs]
    return (xp.reshape(b_p, t_p, d), xs.reshape(b_d, t_d, d),
            stk[0], stk[1], stk[2], stk[3], stk[4], stk[5], stk[6], stk[7])
```

```python
import functools

import jax
import jax.numpy as jnp
from jax import lax
from jax.experimental import pallas as pl
from jax.experimental.pallas import tpu as pltpu

F32 = jnp.float32
BF16 = jnp.bfloat16

NORM_EPS = 1e-6
FFN_RES = 0.5
GLA_HEADS = 4
GLA_DK = 64
GLA_DV = 128
GLA_GATE_RANK = 16
GLA_GATE_TAU = 16.0
GLA_CHUNK = 64
GLA_TILE = 512
SB_HEADS = 8
SB_HD = 64
SB_MASKED_LOGIT = -1e30
SB_AUG = 16
SB_AUG_ONES = 3
SB_QUERY_ROWS = 256
SB_STEPS_PER_TRIP = 6
SB_HEADS_PER_STEP = 4
SB_PAGES_PER_STEP = 16
LOG2E = 1.4426950408889634
X_HEADS = 4
LANE = 128
MXU_DIM = 256
VMEM_LIMIT = 56 * 1024 * 1024


def _cparams(sem):
    return pltpu.CompilerParams(dimension_semantics=sem, vmem_limit_bytes=VMEM_LIMIT)


def _rms(x, w):
    return x * lax.rsqrt(jnp.mean(x * x, axis=-1, keepdims=True) + NORM_EPS) * w


def _softplus(z):
    return jnp.maximum(z, 0.0) + jnp.log(1.0 + jnp.exp(-jnp.abs(z)))


def _log_sigmoid(z):
    return -_softplus(-z)


def _sigmoid(z):
    return 1.0 / (1.0 + jnp.exp(-z))


def _dot(a, b):
    return jnp.dot(a, b, preferred_element_type=F32)


def _dot_nt(a, b):
    return lax.dot_general(a, b, (((1,), (1,)), ((), ())), preferred_element_type=F32)


def _split_bf16(x):
    hi = x.astype(BF16)
    lo = (x - hi.astype(F32)).astype(BF16)
    return hi, lo


def _memkv_kernel(mem_ref, nw_ref, wk_ref, wv_ref, k_ref, v_ref):
    mn = _rms(mem_ref[0], nw_ref[...]).astype(BF16)
    k_ref[0] = _dot(mn, wk_ref[...])
    v_ref[0] = _dot(mn, wv_ref[...])


def _memory_kv(mem, nw, wk, wv):
    b, m, d = mem.shape
    full = lambda shape: pl.BlockSpec(shape, lambda i: (0,) * len(shape))
    row = pl.BlockSpec((1, m, d), lambda i: (i, 0, 0))
    return pl.pallas_call(
        _memkv_kernel,
        grid=(b,),
        in_specs=[row, full((1, d)), full((d, d)), full((d, d))],
        out_specs=[row, row],
        out_shape=[jax.ShapeDtypeStruct((b, m, d), F32)] * 2,
        compiler_params=_cparams(("arbitrary",)),
        name="memory_kv",
    )(mem, nw.reshape(1, d), wk.astype(BF16), wv.astype(BF16))


def _ffn_kernel(x_ref, nw_ref, wg_ref, wu_ref, wd_ref, fw_ref, o_ref, *, final_norm):
    x = x_ref[...]
    h = _rms(x, nw_ref[...]).astype(BF16)
    acc = jnp.zeros_like(x)
    ff = wg_ref.shape[0]
    for c0 in range(0, ff, MXU_DIM):
        sl = slice(c0, min(c0 + MXU_DIM, ff))
        g = _dot_nt(h, wg_ref[sl, :])
        u = _dot_nt(h, wu_ref[sl, :])
        a = (g * _sigmoid(g) * u).astype(BF16)
        acc = acc + _dot(a, wd_ref[sl, :])
    y = x + FFN_RES * acc
    if final_norm:
        y = _rms(y, fw_ref[...])
    o_ref[...] = y


def _ffn_weights(w_gate, w_up, w_down):
    return w_gate.T.astype(BF16), w_up.T.astype(BF16), w_down.astype(BF16)


def _ffn(x, nw, wg, wu, wd, fw, final_norm):
    rows, d = x.shape
    tm = min(rows, 512)
    assert rows % tm == 0
    ff = wg.shape[0]
    const = lambda shape: pl.BlockSpec(shape, lambda i: (0, 0), pipeline_mode=pl.Buffered(1))
    row = pl.BlockSpec((tm, d), lambda i: (i, 0))
    return pl.pallas_call(
        functools.partial(_ffn_kernel, final_norm=final_norm),
        grid=(rows // tm,),
        in_specs=[row, const((1, d)), const((ff, d)), const((ff, d)), const((ff, d)), const((1, d))],
        out_specs=row,
        out_shape=jax.ShapeDtypeStruct((rows, d), F32),
        compiler_params=_cparams(("arbitrary",)),
        name="ffn_final" if final_norm else "ffn",
    )(x, nw.reshape(1, d), wg, wu, wd, fw.reshape(1, d))


NQK = GLA_HEADS * GLA_DK
NV = GLA_HEADS * GLA_DV
NS = SB_HEADS * SB_HD


def _gate_log_decay(pre):
    return _log_sigmoid(pre) * (1.0 / GLA_GATE_TAU)


def _inproj_prompt_kernel(x_ref, nw_ref, wn_ref, wt_ref, wglrt_ref, wg2_ref, bg2_ref, wg2t_ref, bg2t_ref,
                          qg_ref, vg_ref, rg_ref, la_ref, qs_ref,
                          kgt_ref, lat_ref, kst_ref, vst_ref, kstb_ref, vstb_ref):
    h = _rms(x_ref[...], nw_ref[...]).astype(BF16)
    y = _dot(h, wn_ref[...])
    qg_ref[...] = y[:, 0:NQK] * (GLA_DK ** -0.5)
    vg_ref[...] = y[:, NQK:NQK + NV]
    rg_ref[...] = y[:, NQK + NV:NQK + 2 * NV]
    o = NQK + 2 * NV
    qs_ref[...] = (y[:, o:o + NS] * (SB_HD ** -0.5 * LOG2E)).astype(BF16)
    glr = y[:, o + NS:o + NS + LANE].astype(BF16)
    la_ref[...] = _gate_log_decay(_dot(glr, wg2_ref[...]) + bg2_ref[...])
    yt = _dot_nt(wt_ref[...], h)
    kgt_ref[...] = yt[0:NQK]
    kst = yt[NQK:NQK + NS]
    vst = yt[NQK + NS:NQK + 2 * NS]
    kst_ref[...] = kst
    vst_ref[...] = vst
    aug_row = lax.broadcasted_iota(jnp.int32, (SB_AUG, kst.shape[1]), 0)
    ones_rows = jnp.where(aug_row < SB_AUG_ONES, 1.0, 0.0).astype(BF16)
    for hh in range(SB_HEADS):
        base = hh * (SB_HD + SB_AUG)
        kstb_ref[base:base + SB_HD, :] = kst[hh * SB_HD:(hh + 1) * SB_HD].astype(BF16)
        kstb_ref[base + SB_HD:base + SB_HD + SB_AUG, :] = ones_rows
    vstb_ref[...] = vst.astype(BF16)
    glrt = _dot_nt(wglrt_ref[...], h).astype(BF16)
    lat_ref[...] = _gate_log_decay(_dot(wg2t_ref[...], glrt) + bg2t_ref[...])


def _inproj_sample_kernel(x_ref, nw_ref, wn_ref, wt_ref, wglrt_ref, wg2t_ref, bg2t_ref,
                          vg_ref, rg_ref, qs_ref, ks_ref, vs_ref, qgt_ref, kgt_ref, lat_ref):
    h = _rms(x_ref[...], nw_ref[...]).astype(BF16)
    y = _dot(h, wn_ref[...])
    vg_ref[...] = y[:, 0:NV]
    rg_ref[...] = y[:, NV:2 * NV]
    o = 2 * NV
    qs_ref[...] = (y[:, o:o + NS] * (SB_HD ** -0.5 * LOG2E)).astype(BF16)
    ks_ref[...] = y[:, o + NS:o + 2 * NS]
    vs_ref[...] = y[:, o + 2 * NS:o + 3 * NS]
    yt = _dot_nt(wt_ref[...], h)
    qgt_ref[...] = yt[0:NQK] * (GLA_DK ** -0.5)
    kgt_ref[...] = yt[NQK:2 * NQK]
    glrt = _dot_nt(wglrt_ref[...], h).astype(BF16)
    lat_ref[...] = _gate_log_decay(_dot(wg2t_ref[...], glrt) + bg2t_ref[...])


def _prep_inproj_weights(w_in, w_gate2, b_gate2):
    sizes = (NQK, NQK, NV, GLA_GATE_RANK, NV, NS, NS, NS)
    offs = [0]
    for s in sizes:
        offs.append(offs[-1] + s)
    qg, kg, vg, glr, rg, qs, ks, vs = [w_in[:, offs[i]:offs[i + 1]] for i in range(8)]
    glr_pad = jnp.pad(glr, ((0, 0), (0, LANE - GLA_GATE_RANK)))
    cat = lambda parts: jnp.concatenate(parts, axis=1)
    shared = dict(
        wglrt=glr.T.astype(BF16),
        wg2=jnp.pad(w_gate2, ((0, LANE - GLA_GATE_RANK), (0, 0))).astype(BF16),
        bg2=b_gate2.reshape(1, NQK),
        wg2t=w_gate2.T.astype(BF16),
        bg2t=b_gate2.reshape(NQK, 1))
    prompt = dict(wn=cat([qg, vg, rg, qs, glr_pad]).astype(BF16), wt=cat([kg, ks, vs]).T.astype(BF16), **shared)
    sample = dict(wn=cat([vg, rg, qs, ks, vs]).astype(BF16), wt=cat([qg, kg]).T.astype(BF16), **shared)
    return prompt, sample


def _inproj_call(kernel_fn, name, x, inputs, row_outs, col_outs):
    rows, d = x.shape
    tm = min(rows, 512)
    assert rows % tm == 0
    const = lambda a: pl.BlockSpec(a.shape, lambda i: (0, 0), pipeline_mode=pl.Buffered(1))
    sds = jax.ShapeDtypeStruct
    return pl.pallas_call(
        kernel_fn,
        grid=(rows // tm,),
        in_specs=[pl.BlockSpec((tm, d), lambda i: (i, 0))] + [const(a) for a in inputs],
        out_specs=[pl.BlockSpec((tm, n), lambda i: (i, 0)) for n, _ in row_outs]
                  + [pl.BlockSpec((n, tm), lambda i: (0, i)) for n, _ in col_outs],
        out_shape=[sds((rows, n), dt) for n, dt in row_outs] + [sds((n, rows), dt) for n, dt in col_outs],
        compiler_params=_cparams(("arbitrary",)),
        name=name,
    )(x, *inputs)


def _inproj_prompt(x, nw, w):
    inputs = [nw.reshape(1, -1), w["wn"], w["wt"], w["wglrt"], w["wg2"], w["bg2"], w["wg2t"], w["bg2t"]]
    return _inproj_call(_inproj_prompt_kernel, "in_proj_prompt", x, inputs,
                        [(NQK, F32), (NV, F32), (NV, F32), (NQK, F32), (NS, BF16)],
                        [(NQK, F32), (NQK, F32), (NS, F32), (NS, F32), (SB_HEADS * (SB_HD + SB_AUG), BF16),
                         (NS, BF16)])


def _inproj_sample(x, nw, w):
    inputs = [nw.reshape(1, -1), w["wn"], w["wt"], w["wglrt"], w["wg2t"], w["bg2t"]]
    return _inproj_call(_inproj_sample_kernel, "in_proj_sample", x, inputs,
                        [(NV, F32), (NV, F32), (NS, BF16), (NS, F32), (NS, F32)],
                        [(NQK, F32), (NQK, F32), (NQK, F32)])


def _gla_out(o, gnw, r):
    o = o * lax.rsqrt(jnp.mean(o * o, axis=-1, keepdims=True) + NORM_EPS) * gnw
    return o * (r * _sigmoid(r))


def _gla_prompt_kernel(q_ref, v_ref, r_ref, la_ref, kt_ref, lat_ref, gnw_ref, o_ref, s_out_ref, s_ref):
    i = pl.program_id(0)
    c = GLA_CHUNK

    @pl.when(i == 0)
    def _():
        s_ref[...] = jnp.zeros_like(s_ref)

    rr = lax.broadcasted_iota(jnp.int32, (c, c), 0)
    cc = lax.broadcasted_iota(jnp.int32, (c, c), 1)
    causal = cc <= rr
    tri = jnp.where(causal, 1.0, 0.0).astype(BF16)
    trit = jnp.where(rr <= cc, 1.0, 0.0).astype(BF16)
    mid = c // 2
    state = [s_ref[h] for h in range(GLA_HEADS)]

    for ch in range(GLA_TILE // c):
        rows = slice(ch * c, (ch + 1) * c)
        la_hi, la_lo = _split_bf16(la_ref[rows, :])
        b = _dot(tri, la_hi) + _dot(tri, la_lo)
        lat_hi, lat_lo = _split_bf16(lat_ref[:, rows])
        bt = _dot(lat_hi, trit) + _dot(lat_lo, trit)
        q = q_ref[rows, :]
        kt = kt_ref[:, rows]
        qe = q * jnp.exp(b)
        qm = q * jnp.exp(b - b[mid:mid + 1, :])
        ktm = kt * jnp.exp(bt[:, mid:mid + 1] - bt)
        ktd = kt * jnp.exp(bt[:, c - 1:c] - bt)
        dec = jnp.exp(bt[:, c - 1:c])
        for h in range(GLA_HEADS):
            ks = slice(h * GLA_DK, (h + 1) * GLA_DK)
            vs = slice(h * GLA_DV, (h + 1) * GLA_DV)
            s = state[h]
            v = v_ref[rows, vs]
            att = jnp.where(causal, _dot(qm[:, ks], ktm[ks, :]), 0.0)
            o = _dot(qe[:, ks], s) + _dot(att, v)
            state[h] = dec[ks, :] * s + _dot(ktd[ks, :], v)
            o_ref[rows, vs] = _gla_out(o, gnw_ref[h:h + 1, :], r_ref[rows, vs])

    for h in range(GLA_HEADS):
        s_ref[h] = state[h]

    @pl.when(i == pl.num_programs(0) - 1)
    def _():
        s_out_ref[...] = s_ref[...]


def _gla_prompt(qg, vg, rg, la, kgt, lat, gnw):
    t = qg.shape[0]
    nqk = GLA_HEADS * GLA_DK
    nv = GLA_HEADS * GLA_DV
    tg = GLA_TILE
    assert t % tg == 0
    row = lambda n: pl.BlockSpec((tg, n), lambda i: (i, 0))
    col = lambda n: pl.BlockSpec((n, tg), lambda i: (0, i))
    return pl.pallas_call(
        _gla_prompt_kernel,
        grid=(t // tg,),
        in_specs=[row(nqk), row(nv), row(nv), row(nqk), col(nqk), col(nqk),
                  pl.BlockSpec((GLA_HEADS, GLA_DV), lambda i: (0, 0))],
        out_specs=[row(nv), pl.BlockSpec((GLA_HEADS, GLA_DK, GLA_DV), lambda i: (0, 0, 0))],
        out_shape=[jax.ShapeDtypeStruct((t, nv), F32),
                   jax.ShapeDtypeStruct((GLA_HEADS, GLA_DK, GLA_DV), F32)],
        scratch_shapes=[pltpu.VMEM((GLA_HEADS, GLA_DK, GLA_DV), F32)],
        compiler_params=_cparams(("arbitrary",)),
        name="gla_prompt",
    )(qg, vg, rg, la, kgt, lat, gnw)


def _gla_sample_kernel(qt_ref, kt_ref, lat_ref, v_ref, r_ref, s0_ref, gnw_ref, o_ref, s_out_ref):
    n_t = v_ref.shape[1]
    for h in range(GLA_HEADS):
        ks = slice(h * GLA_DK, (h + 1) * GLA_DK)
        vs = slice(h * GLA_DV, (h + 1) * GLA_DV)
        s = s0_ref[0, h]
        for t in range(n_t):
            a = jnp.exp(lat_ref[0, ks, t:t + 1])
            s = a * s + kt_ref[0, ks, t:t + 1] * v_ref[0, t:t + 1, vs]
            o = jnp.sum(qt_ref[0, ks, t:t + 1] * s, axis=0, keepdims=True)
            o_ref[0, t:t + 1, vs] = _gla_out(o, gnw_ref[h:h + 1, :], r_ref[0, t:t + 1, vs])
        s_out_ref[0, h] = s


def _gla_sample(qgt, kgt, lat, vg, rg, s0, gnw):
    b, n_t, nv = vg.shape
    nqk = GLA_HEADS * GLA_DK
    colb = pl.BlockSpec((1, nqk, n_t), lambda i: (i, 0, 0))
    rowb = pl.BlockSpec((1, n_t, nv), lambda i: (i, 0, 0))
    st = pl.BlockSpec((1, GLA_HEADS, GLA_DK, GLA_DV), lambda i: (i, 0, 0, 0))
    return pl.pallas_call(
        _gla_sample_kernel,
        grid=(b,),
        in_specs=[colb, colb, colb, rowb, rowb, st, pl.BlockSpec((GLA_HEADS, GLA_DV), lambda i: (0, 0))],
        out_specs=[rowb, st],
        out_shape=[jax.ShapeDtypeStruct((b, n_t, nv), F32),
                   jax.ShapeDtypeStruct((b, GLA_HEADS, GLA_DK, GLA_DV), F32)],
        compiler_params=_cparams(("arbitrary",)),
        name="gla_sample",
    )(qgt, kgt, lat, vg, rg, s0, gnw)


def _neg_tri(n, dtype):
    rr = lax.broadcasted_iota(jnp.int32, (n, n), 0)
    cc = lax.broadcasted_iota(jnp.int32, (n, n), 1)
    return jnp.where(rr >= cc, -1.0, 0.0).astype(dtype)


def _softplus2(z2):
    sign_bit = jnp.uint32(0x80000000)
    neg_abs = pltpu.bitcast(pltpu.bitcast(z2, jnp.uint32) | sign_bit, F32)
    return jnp.maximum(z2, 0.0) + jnp.log2(1.0 + jnp.exp2(neg_abs))


def _sb_prompt_kernel(bias_ref, q_ref, kt_ref, vt_ref, o_ref, q_sc, z_sc, sp_sc, w_sc, scale_sc, *, qr, blk, hp):
    g = pl.program_id(0)
    i = pl.program_id(1)
    n_diag = qr // blk
    top = (i + 1) * n_diag - 1
    ntri = _neg_tri(blk, BF16)
    aug_col = lax.broadcasted_iota(jnp.int32, (qr, SB_AUG), 1)
    for hh in range(hp):
        b2 = jnp.full((qr, SB_AUG), bias_ref[g * hp + hh] * LOG2E, F32)
        b_hi = b2.astype(BF16).astype(F32)
        bias_cols = jnp.where(aug_col == 0, b_hi, jnp.where(aug_col == 1, b2 - b_hi, 0.0))
        for masked in range(2):
            q_sc[masked, hh, :, 0:SB_HD] = q_ref[:, hh * SB_HD:(hh + 1) * SB_HD]
            mask_val = SB_MASKED_LOGIT if masked else 0.0
            q_sc[masked, hh, :, SB_HD:SB_HD + SB_AUG] = jnp.where(aug_col == 2, mask_val, bias_cols).astype(BF16)
        w_sc[0, hh] = jnp.zeros((qr, blk), BF16)
        scale_sc[0, hh] = jnp.ones((1, qr), F32)

    def key_block(ref, hh, j):
        return ref[hh, :, pl.ds(pl.multiple_of(j * blk, blk), blk)]

    rr = lax.broadcasted_iota(jnp.int32, (qr, blk), 0)
    cc = lax.broadcasted_iota(jnp.int32, (qr, blk), 1)

    def logits_stage(n, slot, hh, diagonal):
        j = jnp.maximum(top - n, 0)
        masked = jnp.where(n <= top, 0, 1)
        z = _dot(q_sc[masked, hh], key_block(kt_ref, hh, j))
        if diagonal:
            keep = cc + (n_diag - 1 - n) * blk < rr
            z = jnp.where(keep, z, SB_MASKED_LOGIT)
            sp = jnp.where(keep, _softplus2(z), 0.0).astype(BF16)
        else:
            sign_bit = jnp.uint32(0x80000000)
            neg_abs = pltpu.bitcast(pltpu.bitcast(z, jnp.uint32) | sign_bit, F32)
            l2 = (jnp.log(1.0 + jnp.exp2(neg_abs)) * LOG2E).astype(BF16)
            sp = jnp.maximum(z.astype(BF16), 0) + l2
        z_sc[slot, hh] = z
        sp_sc[slot, hh] = sp

    def step(n, slot, st, diagonal=False):
        prev = 1 - slot
        j_value = jnp.clip(top - n + 2, 0, top)
        out = []
        for hh in range(hp):
            carry, acc = st[2 * hh], st[2 * hh + 1]
            acc = acc + _dot_nt(key_block(vt_ref, hh, j_value), w_sc[prev, hh]) * scale_sc[prev, hh]
            lsum = _dot(sp_sc[prev, hh], ntri)
            w_sc[slot, hh] = jnp.exp2((z_sc[prev, hh] + lsum).astype(BF16))
            scale_sc[slot, hh] = jnp.exp2(carry)
            carry = carry + lsum[:, 0:LANE].T[0:1, :]
            logits_stage(n, slot, hh, diagonal)
            out.extend((carry, acc))
        return tuple(out)

    state = (jnp.zeros((1, qr), F32), jnp.zeros((SB_HD, qr), F32)) * hp
    for hh in range(hp):
        logits_stage(0, 0, hh, True)
    for n in range(1, n_diag):
        state = step(n, n % 2, state, True)

    spt = SB_STEPS_PER_TRIP
    first = n_diag

    def body(p, st):
        for u in range(spt):
            st = step(spt * p + first + u, (first + u) % 2, st)
        return st

    def pair(n0, st):
        return step(n0 + 1, (first + 1) % 2, step(n0, first % 2, st))

    n_steps = top + 3 - first
    n_trips = n_steps // spt
    state = lax.fori_loop(0, n_trips, body, state)
    for k in range(spt // 2):
        n0 = first + spt * n_trips + 2 * k
        state = lax.cond(n0 <= top + 2, functools.partial(pair, n0), lambda st: st, state)
    for hh in range(hp):
        o_ref[:, hh * SB_HD:(hh + 1) * SB_HD] = state[2 * hh + 1].T


def _sb_prompt(qs, kst, vst, bias):
    t = qs.shape[0]
    blk = min(t, MXU_DIM)
    qr = min(t, SB_QUERY_ROWS)
    assert t % qr == 0 and qr % blk == 0
    hp = SB_HEADS_PER_STEP
    dk = SB_HD + SB_AUG
    whole_seq = lambda rows_per_head: pl.BlockSpec((hp, rows_per_head, t), lambda h, i: (h, 0, 0),
                                                   pipeline_mode=pl.Buffered(1))
    rows = pl.BlockSpec((qr, hp * SB_HD), lambda h, i: (i, h))
    return pl.pallas_call(
        functools.partial(_sb_prompt_kernel, qr=qr, blk=blk, hp=hp),
        grid=(SB_HEADS // hp, t // qr),
        in_specs=[pl.BlockSpec(memory_space=pltpu.SMEM), rows, whole_seq(dk), whole_seq(SB_HD)],
        out_specs=rows,
        out_shape=jax.ShapeDtypeStruct((t, SB_HEADS * SB_HD), F32),
        scratch_shapes=[pltpu.VMEM((2, hp, qr, dk), BF16), pltpu.VMEM((2, hp, qr, blk), F32),
                        pltpu.VMEM((2, hp, qr, blk), BF16), pltpu.VMEM((2, hp, qr, blk), BF16),
                        pltpu.VMEM((2, hp, 1, qr), F32)],
        compiler_params=_cparams(("arbitrary", "arbitrary")),
        name="sb_prompt",
    )(bias, qs, kst.reshape(SB_HEADS, dk, t), vst.reshape(SB_HEADS, SB_HD, t))


def _sb_sample_kernel(pt_ref, q_ref, bias_ref, kn_ref, vn_ref, *rest, n_t, n_pg):
    k_refs = rest[:n_pg]
    v_refs = rest[n_pg:2 * n_pg]
    o_ref, carry_ref, acc_ref = rest[2 * n_pg:]
    p = pl.program_id(1)
    q = q_ref[0]
    nr, hd = q.shape
    bias = bias_ref[...]
    page = k_refs[0].shape[2]

    @pl.when(p == 0)
    def _():
        t_row = lax.broadcasted_iota(jnp.int32, (nr, 1), 0) // SB_HEADS
        carry = jnp.zeros((nr, 1), F32)
        acc = jnp.zeros((nr, hd), F32)
        for s in range(n_t - 2, -1, -1):
            z = jnp.sum(q * kn_ref[0, s:s + 1, :], axis=-1, keepdims=True) + bias
            m = s < t_row
            carry = carry - jnp.where(m, _softplus2(z), 0.0)
            w = jnp.where(m, jnp.exp2(z + carry), 0.0)
            acc = acc + w * vn_ref[0, s:s + 1, :]
        carry_ref[...] = carry
        acc_ref[...] = acc

    qb = q.astype(BF16)
    ntri = _neg_tri(page, BF16)
    zs = [_dot(qb, k_refs[g][0].astype(BF16)) + bias for g in range(n_pg)]
    ls = [_dot(_softplus2(z).astype(BF16), ntri) for z in zs]
    carry = carry_ref[...]
    acc = jnp.zeros((nr, hd), F32)
    for g in range(n_pg):
        w = jnp.exp2(zs[g] + ls[g] + carry)
        acc = acc + _dot_nt(w.astype(BF16), v_refs[g][0].astype(BF16))
        carry = carry + ls[g][:, 0:1]
    carry_ref[...] = carry
    acc_ref[...] += acc

    @pl.when(p == pl.num_programs(1) - 1)
    def _():
        row_head = lax.broadcasted_iota(jnp.int32, (SB_HEADS, hd), 0)
        lane_head = lax.broadcasted_iota(jnp.int32, (SB_HEADS, hd), 1) // SB_HD
        own = (row_head == lane_head)[None]
        a3 = acc_ref[...].reshape(n_t, SB_HEADS, hd)
        o_ref[0] = jnp.sum(jnp.where(own, a3, 0.0), axis=1)


def _sb_sample(qs, ks_new, vs_new, pool_kt, pool_vt, page_table, bias):
    b, n_t, hd = qs.shape
    n_pages = page_table.shape[1]
    page = pool_kt.shape[2]
    n_pg = SB_PAGES_PER_STEP
    while n_pages % n_pg:
        n_pg //= 2
    nr = SB_HEADS * n_t
    head_of_lane = jnp.arange(hd, dtype=jnp.int32) // SB_HD
    head_of_row = jnp.arange(nr, dtype=jnp.int32) % SB_HEADS
    sel = (head_of_row[:, None] == head_of_lane[None, :])
    q_rows = jnp.repeat(qs.astype(F32), SB_HEADS, axis=1)
    q_bd = jnp.where(sel[None], q_rows, 0.0)
    bias_col = (bias[head_of_row] * LOG2E).reshape(nr, 1)

    def page_spec(g):
        return pl.BlockSpec((1, hd, page), lambda i, p, pt: (pt[i, n_pages - 1 - (p * n_pg + g)], 0, 0))

    grid_spec = pltpu.PrefetchScalarGridSpec(
        num_scalar_prefetch=1,
        grid=(b, n_pages // n_pg),
        in_specs=[pl.BlockSpec((1, nr, hd), lambda i, p, pt: (i, 0, 0)),
                  pl.BlockSpec((nr, 1), lambda i, p, pt: (0, 0)),
                  pl.BlockSpec((1, n_t, hd), lambda i, p, pt: (i, 0, 0)),
                  pl.BlockSpec((1, n_t, hd), lambda i, p, pt: (i, 0, 0))]
                 + [page_spec(g) for g in range(n_pg)] * 2,
        out_specs=pl.BlockSpec((1, n_t, hd), lambda i, p, pt: (i, 0, 0)),
        scratch_shapes=[pltpu.VMEM((nr, 1), F32), pltpu.VMEM((nr, hd), F32)],
    )
    return pl.pallas_call(
        functools.partial(_sb_sample_kernel, n_t=n_t, n_pg=n_pg),
        grid_spec=grid_spec,
        out_shape=jax.ShapeDtypeStruct((b, n_t, hd), F32),
        compiler_params=_cparams(("arbitrary", "arbitrary")),
        name="sb_sample",
    )(page_table, q_bd, bias_col, ks_new, vs_new, *([pool_kt] * n_pg), *([pool_vt] * n_pg))


def _mix_residual(x, og, osb, wo_ref):
    ng = og.shape[1]
    return x + _dot(og.astype(BF16), wo_ref[0:ng, :]) + _dot(osb.astype(BF16), wo_ref[ng:, :])


def _xattn_query(x, nw_ref, wcq_ref):
    h = _rms(x, nw_ref[...]).astype(BF16)
    return _dot(h, wcq_ref[...]) * ((x.shape[1] // X_HEADS) ** -0.5)


def _xattn_heads(q, mk_ref, mv_ref):
    hd = q.shape[1] // X_HEADS
    outs = []
    for hh in range(X_HEADS):
        sl = slice(hh * hd, (hh + 1) * hd)
        mk = mk_ref[0, :, sl] if len(mk_ref.shape) == 3 else mk_ref[0, :, hh, :]
        mv = mv_ref[0, :, sl] if len(mv_ref.shape) == 3 else mv_ref[0, :, hh, :]
        s = _dot_nt(q[:, sl], mk)
        s = s - jnp.max(s, axis=-1, keepdims=True)
        e = jnp.exp(s)
        p = e / jnp.sum(e, axis=-1, keepdims=True)
        outs.append(_dot(p, mv))
    return jnp.concatenate(outs, axis=-1)


def _mix_xattn_kernel(x_ref, og_ref, os_ref, wo_ref, nw_ref, wcq_ref, mk_ref, mv_ref, wco_ref, o_ref):
    x = _mix_residual(x_ref[0], og_ref[0], os_ref[0], wo_ref)
    o = _xattn_heads(_xattn_query(x, nw_ref, wcq_ref), mk_ref, mv_ref)
    o_ref[0] = x + _dot(o.astype(BF16), wco_ref[...])


def _mix_query_kernel(x_ref, og_ref, os_ref, wo_ref, nw_ref, wcq_ref, x2_ref, q_ref):
    x = _mix_residual(x_ref[...], og_ref[...], os_ref[...], wo_ref)
    x2_ref[...] = x
    q_ref[...] = _xattn_query(x, nw_ref, wcq_ref)


def _xattn_core_kernel(q_ref, mk_ref, mv_ref, o_ref):
    o_ref[0] = _xattn_heads(q_ref[0], mk_ref, mv_ref)


def _xattn_out_kernel(x_ref, o_ref, wco_ref, y_ref):
    y_ref[...] = x_ref[...] + _dot(o_ref[...].astype(BF16), wco_ref[...])


def _mix_xattn_sample(x, og, osb, wo, nw, wcq, mk, mv, wco):
    b, r, d = x.shape
    rows = b * r
    full = lambda a: pl.BlockSpec(a.shape, lambda *_: (0,) * a.ndim)
    sds = jax.ShapeDtypeStruct
    ins = [x.reshape(rows, d), og.reshape(rows, -1), osb.reshape(rows, -1), wo, nw.reshape(1, d), wcq]
    x2, q = pl.pallas_call(
        _mix_query_kernel, grid=(1,), in_specs=[full(a) for a in ins],
        out_specs=[pl.BlockSpec((rows, d), lambda i: (0, 0))] * 2,
        out_shape=[sds((rows, d), F32)] * 2,
        compiler_params=_cparams(("arbitrary",)), name="mix_query_sample")(*ins)
    per_batch = pl.BlockSpec((1, r, d), lambda i: (i, 0, 0))
    mem = pl.BlockSpec((1,) + mk.shape[1:], lambda i: (i,) + (0,) * (mk.ndim - 1))
    o = pl.pallas_call(
        _xattn_core_kernel, grid=(b,), in_specs=[per_batch, mem, mem], out_specs=per_batch,
        out_shape=sds((b, r, d), F32),
        compiler_params=_cparams(("arbitrary",)), name="xattn_core_sample")(q.reshape(b, r, d), mk, mv)
    ins = [x2, o.reshape(rows, d), wco]
    y = pl.pallas_call(
        _xattn_out_kernel, grid=(1,), in_specs=[full(a) for a in ins],
        out_specs=pl.BlockSpec((rows, d), lambda i: (0, 0)),
        out_shape=sds((rows, d), F32),
        compiler_params=_cparams(("arbitrary",)), name="xattn_out_sample")(*ins)
    return y.reshape(b, r, d)


def _mix_xattn(x, og, osb, wo, nw, wcq, mk, mv, wco):
    b, r, d = x.shape
    tm = min(r, 1024)
    assert r % tm == 0
    ng = og.shape[2]
    ns = osb.shape[2]
    const = lambda shape: pl.BlockSpec(shape, lambda i, j: (0, 0), pipeline_mode=pl.Buffered(1))
    row = lambda n: pl.BlockSpec((1, tm, n), lambda i, j: (i, j, 0))
    mem = pl.BlockSpec((1,) + mk.shape[1:], lambda i, j: (i,) + (0,) * (mk.ndim - 1))
    return pl.pallas_call(
        _mix_xattn_kernel,
        grid=(b, r // tm),
        in_specs=[row(d), row(ng), row(ns), const((ng + ns, d)), const((1, d)), const((d, d)), mem, mem,
                  const((d, d))],
        out_specs=row(d),
        out_shape=jax.ShapeDtypeStruct((b, r, d), F32),
        compiler_params=_cparams(("arbitrary", "arbitrary")),
        name="mix_xattn",
    )(x, og, osb, wo, nw.reshape(1, d), wcq, mk, mv, wco)


def kernel(x_prompt, x_sample, mem_prompt, cache_sb_k, cache_sb_v, page_table, state_gla, cache_mem_k, cache_mem_v, ffn1_norm_w, ffn1_w_gate, ffn1_w_up, ffn1_w_down, mix_norm_w, w_in, w_gate2, b_gate2, gla_norm_w, sb_bias, w_out, xattn_norm_w, mem_norm_w, w_mk, w_mv, w_cq, w_co, ffn2_norm_w, ffn2_w_gate, ffn2_w_up, ffn2_w_down, final_norm_w):
    depth = w_in.shape[0]
    b_p, t_p, d = x_prompt.shape
    b_d, t_d, _ = x_sample.shape
    assert b_p == 1
    n_pool, page = cache_sb_k.shape[1], cache_sb_k.shape[2]
    hd_s = SB_HEADS * SB_HD
    nv = GLA_HEADS * GLA_DV

    xp = x_prompt.reshape(b_p * t_p, d)
    xs = x_sample.reshape(b_d * t_d, d)
    outs = [[] for _ in range(8)]
    for l in range(depth):
        last = l == depth - 1
        f1 = _ffn_weights(ffn1_w_gate[l], ffn1_w_up[l], ffn1_w_down[l])
        f2 = _ffn_weights(ffn2_w_gate[l], ffn2_w_up[l], ffn2_w_down[l])
        wproj_p, wproj_s = _prep_inproj_weights(w_in[l], w_gate2[l], b_gate2[l])
        wo = w_out[l].astype(BF16)
        wcq = w_cq[l].astype(BF16)
        wco = w_co[l].astype(BF16)

        mk_p, mv_p = _memory_kv(mem_prompt, mem_norm_w[l], w_mk[l], w_mv[l])

        x1 = _ffn(xp, ffn1_norm_w[l], *f1, final_norm_w, False)
        qg, vg, rg, la, qs, kgt, lat, kst, vst, kstb, vstb = _inproj_prompt(x1, mix_norm_w[l], wproj_p)
        og, s_p = _gla_prompt(qg, vg, rg, la, kgt, lat, gla_norm_w[l])
        osb = _sb_prompt(qs, kstb, vstb, sb_bias[l])
        x3 = _mix_xattn(x1.reshape(b_p, t_p, d), og.reshape(b_p, t_p, nv), osb.reshape(b_p, t_p, hd_s),
                        wo, xattn_norm_w[l], wcq, mk_p, mv_p, wco)
        xp = _ffn(x3.reshape(b_p * t_p, d), ffn2_norm_w[l], *f2, final_norm_w, last)
        seq_major = lambda a: a.reshape(b_p, SB_HEADS, SB_HD, t_p).transpose(0, 3, 1, 2)
        outs[0].append(seq_major(kst))
        outs[1].append(seq_major(vst))
        outs[2].append(s_p.reshape(b_p, GLA_HEADS, GLA_DK, GLA_DV))
        outs[3].append(mk_p.reshape(b_p, -1, X_HEADS, d // X_HEADS))
        outs[4].append(mv_p.reshape(b_p, -1, X_HEADS, d // X_HEADS))

        x1 = _ffn(xs, ffn1_norm_w[l], *f1, final_norm_w, False)
        vg, rg, qs, ks, vs, qgt, kgt, lat = _inproj_sample(x1, mix_norm_w[l], wproj_s)
        tok_major = lambda a: a.reshape(a.shape[0], b_d, t_d).transpose(1, 0, 2)
        feat_major = lambda c: c.transpose(0, 2, 3, 1).reshape(n_pool, hd_s, page)
        og, s_s = _gla_sample(tok_major(qgt), tok_major(kgt), tok_major(lat),
                              vg.reshape(b_d, t_d, nv), rg.reshape(b_d, t_d, nv), state_gla[l], gla_norm_w[l])
        osb = _sb_sample(qs.reshape(b_d, t_d, hd_s), ks.reshape(b_d, t_d, hd_s), vs.reshape(b_d, t_d, hd_s),
                         feat_major(cache_sb_k[l]), feat_major(cache_sb_v[l]), page_table, sb_bias[l])
        x3 = _mix_xattn_sample(x1.reshape(b_d, t_d, d), og, osb, wo, xattn_norm_w[l], wcq,
                               cache_mem_k[l], cache_mem_v[l], wco)
        xs = _ffn(x3.reshape(b_d * t_d, d), ffn2_norm_w[l], *f2, final_norm_w, last)
        outs[5].append(ks.reshape(b_d, t_d, SB_HEADS, SB_HD))
        outs[6].append(vs.reshape(b_d, t_d, SB_HEADS, SB_HD))
        outs[7].append(s_s)

    stk = [jnp.stack(o) for o in outs]
    return (xp.reshape(b_p, t_p, d), xs.reshape(b_d, t_d, d),
            stk[0], stk[1], stk[2], stk[3], stk[4], stk[5], stk[6], stk[7])
```

```python
import functools

import jax
import jax.numpy as jnp
from jax import lax
from jax.experimental import pallas as pl
from jax.experimental.pallas import tpu as pltpu

F32 = jnp.float32
BF16 = jnp.bfloat16

NORM_EPS = 1e-6
FFN_RES = 0.5
GLA_HEADS = 4
GLA_DK = 64
GLA_DV = 128
GLA_GATE_RANK = 16
GLA_GATE_TAU = 16.0
GLA_CHUNK = 64
GLA_TILE = 512
SB_HEADS = 8
SB_HD = 64
SB_MASKED_LOGIT = -1e30
SB_AUG = 16
SB_AUG_ONES = 3
SB_QUERY_ROWS = 256
SB_STEPS_PER_TRIP = 4
SB_HEADS_PER_STEP = 4
SB_PAGES_PER_STEP = 16
LOG2E = 1.4426950408889634
X_HEADS = 4
LANE = 128
MXU_DIM = 256
VMEM_LIMIT = 56 * 1024 * 1024


def _cparams(sem):
    return pltpu.CompilerParams(dimension_semantics=sem, vmem_limit_bytes=VMEM_LIMIT)


def _rms(x, w):
    return x * lax.rsqrt(jnp.mean(x * x, axis=-1, keepdims=True) + NORM_EPS) * w


def _softplus(z):
    return jnp.maximum(z, 0.0) + jnp.log(1.0 + jnp.exp(-jnp.abs(z)))


def _log_sigmoid(z):
    return -_softplus(-z)


def _sigmoid(z):
    return 1.0 / (1.0 + jnp.exp(-z))


def _dot(a, b):
    return jnp.dot(a, b, preferred_element_type=F32)


def _dot_nt(a, b):
    return lax.dot_general(a, b, (((1,), (1,)), ((), ())), preferred_element_type=F32)


def _split_bf16(x):
    hi = x.astype(BF16)
    lo = (x - hi.astype(F32)).astype(BF16)
    return hi, lo


def _memkv_kernel(mem_ref, nw_ref, wk_ref, wv_ref, k_ref, v_ref):
    mn = _rms(mem_ref[0], nw_ref[...]).astype(BF16)
    k_ref[0] = _dot(mn, wk_ref[...])
    v_ref[0] = _dot(mn, wv_ref[...])


def _memory_kv(mem, nw, wk, wv):
    b, m, d = mem.shape
    full = lambda shape: pl.BlockSpec(shape, lambda i: (0,) * len(shape))
    row = pl.BlockSpec((1, m, d), lambda i: (i, 0, 0))
    return pl.pallas_call(
        _memkv_kernel,
        grid=(b,),
        in_specs=[row, full((1, d)), full((d, d)), full((d, d))],
        out_specs=[row, row],
        out_shape=[jax.ShapeDtypeStruct((b, m, d), F32)] * 2,
        compiler_params=_cparams(("arbitrary",)),
        name="memory_kv",
    )(mem, nw.reshape(1, d), wk.astype(BF16), wv.astype(BF16))


def _ffn_kernel(x_ref, nw_ref, wg_ref, wu_ref, wd_ref, fw_ref, o_ref, *, final_norm):
    x = x_ref[...]
    h = _rms(x, nw_ref[...]).astype(BF16)
    acc = jnp.zeros_like(x)
    ff = wg_ref.shape[0]
    for c0 in range(0, ff, MXU_DIM):
        sl = slice(c0, min(c0 + MXU_DIM, ff))
        g = _dot_nt(h, wg_ref[sl, :])
        u = _dot_nt(h, wu_ref[sl, :])
        a = (g * _sigmoid(g) * u).astype(BF16)
        acc = acc + _dot(a, wd_ref[sl, :])
    y = x + FFN_RES * acc
    if final_norm:
        y = _rms(y, fw_ref[...])
    o_ref[...] = y


def _ffn_weights(w_gate, w_up, w_down):
    return w_gate.T.astype(BF16), w_up.T.astype(BF16), w_down.astype(BF16)


def _ffn(x, nw, wg, wu, wd, fw, final_norm):
    rows, d = x.shape
    tm = min(rows, 512)
    assert rows % tm == 0
    ff = wg.shape[0]
    const = lambda shape: pl.BlockSpec(shape, lambda i: (0, 0), pipeline_mode=pl.Buffered(1))
    row = pl.BlockSpec((tm, d), lambda i: (i, 0))
    return pl.pallas_call(
        functools.partial(_ffn_kernel, final_norm=final_norm),
        grid=(rows // tm,),
        in_specs=[row, const((1, d)), const((ff, d)), const((ff, d)), const((ff, d)), const((1, d))],
        out_specs=row,
        out_shape=jax.ShapeDtypeStruct((rows, d), F32),
        compiler_params=_cparams(("arbitrary",)),
        name="ffn_final" if final_norm else "ffn",
    )(x, nw.reshape(1, d), wg, wu, wd, fw.reshape(1, d))


NQK = GLA_HEADS * GLA_DK
NV = GLA_HEADS * GLA_DV
NS = SB_HEADS * SB_HD


def _gate_log_decay(pre):
    return _log_sigmoid(pre) * (1.0 / GLA_GATE_TAU)


def _inproj_prompt_kernel(x_ref, nw_ref, wn_ref, wt_ref, wglrt_ref, wg2_ref, bg2_ref, wg2t_ref, bg2t_ref,
                          qg_ref, vg_ref, rg_ref, la_ref, qs_ref,
                          kgt_ref, lat_ref, kst_ref, vst_ref, kstb_ref, vstb_ref):
    h = _rms(x_ref[...], nw_ref[...]).astype(BF16)
    y = _dot(h, wn_ref[...])
    qg_ref[...] = y[:, 0:NQK] * (GLA_DK ** -0.5)
    vg_ref[...] = y[:, NQK:NQK + NV]
    rg_ref[...] = y[:, NQK + NV:NQK + 2 * NV]
    o = NQK + 2 * NV
    qs_ref[...] = (y[:, o:o + NS] * (SB_HD ** -0.5 * LOG2E)).astype(BF16)
    glr = y[:, o + NS:o + NS + LANE].astype(BF16)
    la_ref[...] = _gate_log_decay(_dot(glr, wg2_ref[...]) + bg2_ref[...])
    yt = _dot_nt(wt_ref[...], h)
    kgt_ref[...] = yt[0:NQK]
    kst = yt[NQK:NQK + NS]
    vst = yt[NQK + NS:NQK + 2 * NS]
    kst_ref[...] = kst
    vst_ref[...] = vst
    aug_row = lax.broadcasted_iota(jnp.int32, (SB_AUG, kst.shape[1]), 0)
    ones_rows = jnp.where(aug_row < SB_AUG_ONES, 1.0, 0.0).astype(BF16)
    for hh in range(SB_HEADS):
        base = hh * (SB_HD + SB_AUG)
        kstb_ref[base:base + SB_HD, :] = kst[hh * SB_HD:(hh + 1) * SB_HD].astype(BF16)
        kstb_ref[base + SB_HD:base + SB_HD + SB_AUG, :] = ones_rows
    vstb_ref[...] = vst.astype(BF16)
    glrt = _dot_nt(wglrt_ref[...], h).astype(BF16)
    lat_ref[...] = _gate_log_decay(_dot(wg2t_ref[...], glrt) + bg2t_ref[...])


def _inproj_sample_kernel(x_ref, nw_ref, wn_ref, wt_ref, wglrt_ref, wg2t_ref, bg2t_ref,
                          vg_ref, rg_ref, qs_ref, ks_ref, vs_ref, qgt_ref, kgt_ref, lat_ref):
    h = _rms(x_ref[...], nw_ref[...]).astype(BF16)
    y = _dot(h, wn_ref[...])
    vg_ref[...] = y[:, 0:NV]
    rg_ref[...] = y[:, NV:2 * NV]
    o = 2 * NV
    qs_ref[...] = (y[:, o:o + NS] * (SB_HD ** -0.5 * LOG2E)).astype(BF16)
    ks_ref[...] = y[:, o + NS:o + 2 * NS]
    vs_ref[...] = y[:, o + 2 * NS:o + 3 * NS]
    yt = _dot_nt(wt_ref[...], h)
    qgt_ref[...] = yt[0:NQK] * (GLA_DK ** -0.5)
    kgt_ref[...] = yt[NQK:2 * NQK]
    glrt = _dot_nt(wglrt_ref[...], h).astype(BF16)
    lat_ref[...] = _gate_log_decay(_dot(wg2t_ref[...], glrt) + bg2t_ref[...])


def _prep_inproj_weights(w_in, w_gate2, b_gate2):
    sizes = (NQK, NQK, NV, GLA_GATE_RANK, NV, NS, NS, NS)
    offs = [0]
    for s in sizes:
        offs.append(offs[-1] + s)
    qg, kg, vg, glr, rg, qs, ks, vs = [w_in[:, offs[i]:offs[i + 1]] for i in range(8)]
    glr_pad = jnp.pad(glr, ((0, 0), (0, LANE - GLA_GATE_RANK)))
    cat = lambda parts: jnp.concatenate(parts, axis=1)
    shared = dict(
        wglrt=glr.T.astype(BF16),
        wg2=jnp.pad(w_gate2, ((0, LANE - GLA_GATE_RANK), (0, 0))).astype(BF16),
        bg2=b_gate2.reshape(1, NQK),
        wg2t=w_gate2.T.astype(BF16),
        bg2t=b_gate2.reshape(NQK, 1))
    prompt = dict(wn=cat([qg, vg, rg, qs, glr_pad]).astype(BF16), wt=cat([kg, ks, vs]).T.astype(BF16), **shared)
    sample = dict(wn=cat([vg, rg, qs, ks, vs]).astype(BF16), wt=cat([qg, kg]).T.astype(BF16), **shared)
    return prompt, sample


def _inproj_call(kernel_fn, name, x, inputs, row_outs, col_outs):
    rows, d = x.shape
    tm = min(rows, 512)
    assert rows % tm == 0
    const = lambda a: pl.BlockSpec(a.shape, lambda i: (0, 0), pipeline_mode=pl.Buffered(1))
    sds = jax.ShapeDtypeStruct
    return pl.pallas_call(
        kernel_fn,
        grid=(rows // tm,),
        in_specs=[pl.BlockSpec((tm, d), lambda i: (i, 0))] + [const(a) for a in inputs],
        out_specs=[pl.BlockSpec((tm, n), lambda i: (i, 0)) for n, _ in row_outs]
                  + [pl.BlockSpec((n, tm), lambda i: (0, i)) for n, _ in col_outs],
        out_shape=[sds((rows, n), dt) for n, dt in row_outs] + [sds((n, rows), dt) for n, dt in col_outs],
        compiler_params=_cparams(("arbitrary",)),
        name=name,
    )(x, *inputs)


def _inproj_prompt(x, nw, w):
    inputs = [nw.reshape(1, -1), w["wn"], w["wt"], w["wglrt"], w["wg2"], w["bg2"], w["wg2t"], w["bg2t"]]
    return _inproj_call(_inproj_prompt_kernel, "in_proj_prompt", x, inputs,
                        [(NQK, F32), (NV, F32), (NV, F32), (NQK, F32), (NS, BF16)],
                        [(NQK, F32), (NQK, F32), (NS, F32), (NS, F32), (SB_HEADS * (SB_HD + SB_AUG), BF16),
                         (NS, BF16)])


def _inproj_sample(x, nw, w):
    inputs = [nw.reshape(1, -1), w["wn"], w["wt"], w["wglrt"], w["wg2t"], w["bg2t"]]
    return _inproj_call(_inproj_sample_kernel, "in_proj_sample", x, inputs,
                        [(NV, F32), (NV, F32), (NS, BF16), (NS, F32), (NS, F32)],
                        [(NQK, F32), (NQK, F32), (NQK, F32)])


def _gla_out(o, gnw, r):
    o = o * lax.rsqrt(jnp.mean(o * o, axis=-1, keepdims=True) + NORM_EPS) * gnw
    return o * (r * _sigmoid(r))


def _gla_prompt_kernel(q_ref, v_ref, r_ref, la_ref, kt_ref, lat_ref, gnw_ref, o_ref, s_out_ref, s_ref):
    i = pl.program_id(0)
    c = GLA_CHUNK

    @pl.when(i == 0)
    def _():
        s_ref[...] = jnp.zeros_like(s_ref)

    rr = lax.broadcasted_iota(jnp.int32, (c, c), 0)
    cc = lax.broadcasted_iota(jnp.int32, (c, c), 1)
    causal = cc <= rr
    tri = jnp.where(causal, 1.0, 0.0).astype(BF16)
    trit = jnp.where(rr <= cc, 1.0, 0.0).astype(BF16)
    mid = c // 2
    state = [s_ref[h] for h in range(GLA_HEADS)]

    for ch in range(GLA_TILE // c):
        rows = slice(ch * c, (ch + 1) * c)
        la_hi, la_lo = _split_bf16(la_ref[rows, :])
        b = _dot(tri, la_hi) + _dot(tri, la_lo)
        lat_hi, lat_lo = _split_bf16(lat_ref[:, rows])
        bt = _dot(lat_hi, trit) + _dot(lat_lo, trit)
        q = q_ref[rows, :]
        kt = kt_ref[:, rows]
        qe = q * jnp.exp(b)
        qm = q * jnp.exp(b - b[mid:mid + 1, :])
        ktm = kt * jnp.exp(bt[:, mid:mid + 1] - bt)
        ktd = kt * jnp.exp(bt[:, c - 1:c] - bt)
        dec = jnp.exp(bt[:, c - 1:c])
        for h in range(GLA_HEADS):
            ks = slice(h * GLA_DK, (h + 1) * GLA_DK)
            vs = slice(h * GLA_DV, (h + 1) * GLA_DV)
            s = state[h]
            v = v_ref[rows, vs]
            att = jnp.where(causal, _dot(qm[:, ks], ktm[ks, :]), 0.0)
            o = _dot(qe[:, ks], s) + _dot(att, v)
            state[h] = dec[ks, :] * s + _dot(ktd[ks, :], v)
            o_ref[rows, vs] = _gla_out(o, gnw_ref[h:h + 1, :], r_ref[rows, vs])

    for h in range(GLA_HEADS):
        s_ref[h] = state[h]

    @pl.when(i == pl.num_programs(0) - 1)
    def _():
        s_out_ref[...] = s_ref[...]


def _gla_prompt(qg, vg, rg, la, kgt, lat, gnw):
    t = qg.shape[0]
    nqk = GLA_HEADS * GLA_DK
    nv = GLA_HEADS * GLA_DV
    tg = GLA_TILE
    assert t % tg == 0
    row = lambda n: pl.BlockSpec((tg, n), lambda i: (i, 0))
    col = lambda n: pl.BlockSpec((n, tg), lambda i: (0, i))
    return pl.pallas_call(
        _gla_prompt_kernel,
        grid=(t // tg,),
        in_specs=[row(nqk), row(nv), row(nv), row(nqk), col(nqk), col(nqk),
                  pl.BlockSpec((GLA_HEADS, GLA_DV), lambda i: (0, 0))],
        out_specs=[row(nv), pl.BlockSpec((GLA_HEADS, GLA_DK, GLA_DV), lambda i: (0, 0, 0))],
        out_shape=[jax.ShapeDtypeStruct((t, nv), F32),
                   jax.ShapeDtypeStruct((GLA_HEADS, GLA_DK, GLA_DV), F32)],
        scratch_shapes=[pltpu.VMEM((GLA_HEADS, GLA_DK, GLA_DV), F32)],
        compiler_params=_cparams(("arbitrary",)),
        name="gla_prompt",
    )(qg, vg, rg, la, kgt, lat, gnw)


def _gla_sample_kernel(qt_ref, kt_ref, lat_ref, v_ref, r_ref, s0_ref, gnw_ref, o_ref, s_out_ref):
    n_t = v_ref.shape[1]
    for h in range(GLA_HEADS):
        ks = slice(h * GLA_DK, (h + 1) * GLA_DK)
        vs = slice(h * GLA_DV, (h + 1) * GLA_DV)
        s = s0_ref[0, h]
        for t in range(n_t):
            a = jnp.exp(lat_ref[0, ks, t:t + 1])
            s = a * s + kt_ref[0, ks, t:t + 1] * v_ref[0, t:t + 1, vs]
            o = jnp.sum(qt_ref[0, ks, t:t + 1] * s, axis=0, keepdims=True)
            o_ref[0, t:t + 1, vs] = _gla_out(o, gnw_ref[h:h + 1, :], r_ref[0, t:t + 1, vs])
        s_out_ref[0, h] = s


def _gla_sample(qgt, kgt, lat, vg, rg, s0, gnw):
    b, n_t, nv = vg.shape
    nqk = GLA_HEADS * GLA_DK
    colb = pl.BlockSpec((1, nqk, n_t), lambda i: (i, 0, 0))
    rowb = pl.BlockSpec((1, n_t, nv), lambda i: (i, 0, 0))
    st = pl.BlockSpec((1, GLA_HEADS, GLA_DK, GLA_DV), lambda i: (i, 0, 0, 0))
    return pl.pallas_call(
        _gla_sample_kernel,
        grid=(b,),
        in_specs=[colb, colb, colb, rowb, rowb, st, pl.BlockSpec((GLA_HEADS, GLA_DV), lambda i: (0, 0))],
        out_specs=[rowb, st],
        out_shape=[jax.ShapeDtypeStruct((b, n_t, nv), F32),
                   jax.ShapeDtypeStruct((b, GLA_HEADS, GLA_DK, GLA_DV), F32)],
        compiler_params=_cparams(("arbitrary",)),
        name="gla_sample",
    )(qgt, kgt, lat, vg, rg, s0, gnw)


def _neg_tri(n, dtype):
    rr = lax.broadcasted_iota(jnp.int32, (n, n), 0)
    cc = lax.broadcasted_iota(jnp.int32, (n, n), 1)
    return jnp.where(rr >= cc, -1.0, 0.0).astype(dtype)


def _softplus2(z2):
    sign_bit = jnp.uint32(0x80000000)
    neg_abs = pltpu.bitcast(pltpu.bitcast(z2, jnp.uint32) | sign_bit, F32)
    return jnp.maximum(z2, 0.0) + jnp.log2(1.0 + jnp.exp2(neg_abs))


def _sb_prompt_kernel(bias_ref, q_ref, kt_ref, vt_ref, o_ref, q_sc, z_sc, sp_sc, w_sc, scale_sc, *, qr, blk, hp):
    g = pl.program_id(0)
    i = pl.program_id(1)
    n_diag = qr // blk
    top = (i + 1) * n_diag - 1
    ntri = _neg_tri(blk, BF16)
    aug_col = lax.broadcasted_iota(jnp.int32, (qr, SB_AUG), 1)
    for hh in range(hp):
        b2 = jnp.full((qr, SB_AUG), bias_ref[g * hp + hh] * LOG2E, F32)
        b_hi = b2.astype(BF16).astype(F32)
        bias_cols = jnp.where(aug_col == 0, b_hi, jnp.where(aug_col == 1, b2 - b_hi, 0.0))
        for masked in range(2):
            q_sc[masked, hh, :, 0:SB_HD] = q_ref[:, hh * SB_HD:(hh + 1) * SB_HD]
            mask_val = SB_MASKED_LOGIT if masked else 0.0
            q_sc[masked, hh, :, SB_HD:SB_HD + SB_AUG] = jnp.where(aug_col == 2, mask_val, bias_cols).astype(BF16)
        w_sc[0, hh] = jnp.zeros((qr, blk), BF16)
        scale_sc[0, hh] = jnp.ones((1, qr), F32)

    def key_block(ref, hh, j):
        return ref[hh, :, pl.ds(pl.multiple_of(j * blk, blk), blk)]

    rr = lax.broadcasted_iota(jnp.int32, (qr, blk), 0)
    cc = lax.broadcasted_iota(jnp.int32, (qr, blk), 1)

    def logits_stage(n, slot, hh, diagonal):
        j = jnp.maximum(top - n, 0)
        masked = jnp.where(n <= top, 0, 1)
        z = _dot(q_sc[masked, hh], key_block(kt_ref, hh, j))
        if diagonal:
            keep = cc + (n_diag - 1 - n) * blk < rr
            z = jnp.where(keep, z, SB_MASKED_LOGIT)
            sp = jnp.where(keep, _softplus2(z), 0.0).astype(BF16)
        else:
            sign_bit = jnp.uint32(0x80000000)
            neg_abs = pltpu.bitcast(pltpu.bitcast(z, jnp.uint32) | sign_bit, F32)
            l2 = (jnp.log(1.0 + jnp.exp2(neg_abs)) * LOG2E).astype(BF16)
            sp = jnp.maximum(z.astype(BF16), 0) + l2
        z_sc[slot, hh] = z
        sp_sc[slot, hh] = sp

    def step(n, slot, st, diagonal=False):
        prev = 1 - slot
        j_value = jnp.clip(top - n + 2, 0, top)
        out = []
        for hh in range(hp):
            carry, acc = st[2 * hh], st[2 * hh + 1]
            acc = acc + _dot_nt(key_block(vt_ref, hh, j_value), w_sc[prev, hh]) * scale_sc[prev, hh]
            lsum = _dot(sp_sc[prev, hh], ntri)
            w_sc[slot, hh] = jnp.exp2((z_sc[prev, hh] + lsum).astype(BF16))
            scale_sc[slot, hh] = jnp.exp2(carry)
            carry = carry + lsum[:, 0:LANE].T[0:1, :]
            logits_stage(n, slot, hh, diagonal)
            out.extend((carry, acc))
        return tuple(out)

    state = (jnp.zeros((1, qr), F32), jnp.zeros((SB_HD, qr), F32)) * hp
    for hh in range(hp):
        logits_stage(0, 0, hh, True)
    for n in range(1, n_diag):
        state = step(n, n % 2, state, True)

    spt = SB_STEPS_PER_TRIP
    first = n_diag

    def body(p, st):
        for u in range(spt):
            st = step(spt * p + first + u, (first + u) % 2, st)
        return st

    def pair(n0, st):
        return step(n0 + 1, (first + 1) % 2, step(n0, first % 2, st))

    n_steps = top + 3 - first
    n_trips = n_steps // spt
    state = lax.fori_loop(0, n_trips, body, state)
    for k in range(spt // 2):
        n0 = first + spt * n_trips + 2 * k
        state = lax.cond(n0 <= top + 2, functools.partial(pair, n0), lambda st: st, state)
    for hh in range(hp):
        o_ref[:, hh * SB_HD:(hh + 1) * SB_HD] = state[2 * hh + 1].T


def _sb_prompt(qs, kst, vst, bias):
    t = qs.shape[0]
    blk = min(t, MXU_DIM)
    qr = min(t, SB_QUERY_ROWS)
    assert t % qr == 0 and qr % blk == 0
    hp = SB_HEADS_PER_STEP
    dk = SB_HD + SB_AUG
    whole_seq = lambda rows_per_head: pl.BlockSpec((hp, rows_per_head, t), lambda h, i: (h, 0, 0),
                                                   pipeline_mode=pl.Buffered(1))
    rows = pl.BlockSpec((qr, hp * SB_HD), lambda h, i: (i, h))
    return pl.pallas_call(
        functools.partial(_sb_prompt_kernel, qr=qr, blk=blk, hp=hp),
        grid=(SB_HEADS // hp, t // qr),
        in_specs=[pl.BlockSpec(memory_space=pltpu.SMEM), rows, whole_seq(dk), whole_seq(SB_HD)],
        out_specs=rows,
        out_shape=jax.ShapeDtypeStruct((t, SB_HEADS * SB_HD), F32),
        scratch_shapes=[pltpu.VMEM((2, hp, qr, dk), BF16), pltpu.VMEM((2, hp, qr, blk), F32),
                        pltpu.VMEM((2, hp, qr, blk), BF16), pltpu.VMEM((2, hp, qr, blk), BF16),
                        pltpu.VMEM((2, hp, 1, qr), F32)],
        compiler_params=_cparams(("arbitrary", "arbitrary")),
        name="sb_prompt",
    )(bias, qs, kst.reshape(SB_HEADS, dk, t), vst.reshape(SB_HEADS, SB_HD, t))


def _sb_sample_kernel(pt_ref, q_ref, bias_ref, kn_ref, vn_ref, *rest, n_t, n_pg):
    k_refs = rest[:n_pg]
    v_refs = rest[n_pg:2 * n_pg]
    o_ref, carry_ref, acc_ref = rest[2 * n_pg:]
    p = pl.program_id(1)
    q = q_ref[0]
    nr, hd = q.shape
    bias = bias_ref[...]
    page = k_refs[0].shape[2]

    @pl.when(p == 0)
    def _():
        t_row = lax.broadcasted_iota(jnp.int32, (nr, 1), 0) // SB_HEADS
        carry = jnp.zeros((nr, 1), F32)
        acc = jnp.zeros((nr, hd), F32)
        for s in range(n_t - 2, -1, -1):
            z = jnp.sum(q * kn_ref[0, s:s + 1, :], axis=-1, keepdims=True) + bias
            m = s < t_row
            carry = carry - jnp.where(m, _softplus2(z), 0.0)
            w = jnp.where(m, jnp.exp2(z + carry), 0.0)
            acc = acc + w * vn_ref[0, s:s + 1, :]
        carry_ref[...] = carry
        acc_ref[...] = acc

    qb = q.astype(BF16)
    ntri = _neg_tri(page, BF16)
    zs = [_dot(qb, k_refs[g][0].astype(BF16)) + bias for g in range(n_pg)]
    ls = [_dot(_softplus2(z).astype(BF16), ntri) for z in zs]
    carry = carry_ref[...]
    acc = jnp.zeros((nr, hd), F32)
    for g in range(n_pg):
        w = jnp.exp2(zs[g] + ls[g] + carry)
        acc = acc + _dot_nt(w.astype(BF16), v_refs[g][0].astype(BF16))
        carry = carry + ls[g][:, 0:1]
    carry_ref[...] = carry
    acc_ref[...] += acc

    @pl.when(p == pl.num_programs(1) - 1)
    def _():
        row_head = lax.broadcasted_iota(jnp.int32, (SB_HEADS, hd), 0)
        lane_head = lax.broadcasted_iota(jnp.int32, (SB_HEADS, hd), 1) // SB_HD
        own = (row_head == lane_head)[None]
        a3 = acc_ref[...].reshape(n_t, SB_HEADS, hd)
        o_ref[0] = jnp.sum(jnp.where(own, a3, 0.0), axis=1)


def _sb_sample(qs, ks_new, vs_new, pool_kt, pool_vt, page_table, bias):
    b, n_t, hd = qs.shape
    n_pages = page_table.shape[1]
    page = pool_kt.shape[2]
    n_pg = SB_PAGES_PER_STEP
    while n_pages % n_pg:
        n_pg //= 2
    nr = SB_HEADS * n_t
    head_of_lane = jnp.arange(hd, dtype=jnp.int32) // SB_HD
    head_of_row = jnp.arange(nr, dtype=jnp.int32) % SB_HEADS
    sel = (head_of_row[:, None] == head_of_lane[None, :])
    q_rows = jnp.repeat(qs.astype(F32), SB_HEADS, axis=1)
    q_bd = jnp.where(sel[None], q_rows, 0.0)
    bias_col = (bias[head_of_row] * LOG2E).reshape(nr, 1)

    def page_spec(g):
        return pl.BlockSpec((1, hd, page), lambda i, p, pt: (pt[i, n_pages - 1 - (p * n_pg + g)], 0, 0))

    grid_spec = pltpu.PrefetchScalarGridSpec(
        num_scalar_prefetch=1,
        grid=(b, n_pages // n_pg),
        in_specs=[pl.BlockSpec((1, nr, hd), lambda i, p, pt: (i, 0, 0)),
                  pl.BlockSpec((nr, 1), lambda i, p, pt: (0, 0)),
                  pl.BlockSpec((1, n_t, hd), lambda i, p, pt: (i, 0, 0)),
                  pl.BlockSpec((1, n_t, hd), lambda i, p, pt: (i, 0, 0))]
                 + [page_spec(g) for g in range(n_pg)] * 2,
        out_specs=pl.BlockSpec((1, n_t, hd), lambda i, p, pt: (i, 0, 0)),
        scratch_shapes=[pltpu.VMEM((nr, 1), F32), pltpu.VMEM((nr, hd), F32)],
    )
    return pl.pallas_call(
        functools.partial(_sb_sample_kernel, n_t=n_t, n_pg=n_pg),
        grid_spec=grid_spec,
        out_shape=jax.ShapeDtypeStruct((b, n_t, hd), F32),
        compiler_params=_cparams(("arbitrary", "arbitrary")),
        name="sb_sample",
    )(page_table, q_bd, bias_col, ks_new, vs_new, *([pool_kt] * n_pg), *([pool_vt] * n_pg))


def _mix_residual(x, og, osb, wo_ref):
    ng = og.shape[1]
    return x + _dot(og.astype(BF16), wo_ref[0:ng, :]) + _dot(osb.astype(BF16), wo_ref[ng:, :])


def _xattn_query(x, nw_ref, wcq_ref):
    h = _rms(x, nw_ref[...]).astype(BF16)
    return _dot(h, wcq_ref[...]) * ((x.shape[1] // X_HEADS) ** -0.5)


def _xattn_heads(q, mk_ref, mv_ref):
    hd = q.shape[1] // X_HEADS
    outs = []
    for hh in range(X_HEADS):
        sl = slice(hh * hd, (hh + 1) * hd)
        mk = mk_ref[0, :, sl] if len(mk_ref.shape) == 3 else mk_ref[0, :, hh, :]
        mv = mv_ref[0, :, sl] if len(mv_ref.shape) == 3 else mv_ref[0, :, hh, :]
        s = _dot_nt(q[:, sl], mk)
        s = s - jnp.max(s, axis=-1, keepdims=True)
        e = jnp.exp(s)
        p = e / jnp.sum(e, axis=-1, keepdims=True)
        outs.append(_dot(p, mv))
    return jnp.concatenate(outs, axis=-1)


def _mix_xattn_kernel(x_ref, og_ref, os_ref, wo_ref, nw_ref, wcq_ref, mk_ref, mv_ref, wco_ref, o_ref):
    x = _mix_residual(x_ref[0], og_ref[0], os_ref[0], wo_ref)
    o = _xattn_heads(_xattn_query(x, nw_ref, wcq_ref), mk_ref, mv_ref)
    o_ref[0] = x + _dot(o.astype(BF16), wco_ref[...])


def _mix_query_kernel(x_ref, og_ref, os_ref, wo_ref, nw_ref, wcq_ref, x2_ref, q_ref):
    x = _mix_residual(x_ref[...], og_ref[...], os_ref[...], wo_ref)
    x2_ref[...] = x
    q_ref[...] = _xattn_query(x, nw_ref, wcq_ref)


def _xattn_core_kernel(q_ref, mk_ref, mv_ref, o_ref):
    o_ref[0] = _xattn_heads(q_ref[0], mk_ref, mv_ref)


def _xattn_out_kernel(x_ref, o_ref, wco_ref, y_ref):
    y_ref[...] = x_ref[...] + _dot(o_ref[...].astype(BF16), wco_ref[...])


def _mix_xattn_sample(x, og, osb, wo, nw, wcq, mk, mv, wco):
    b, r, d = x.shape
    rows = b * r
    full = lambda a: pl.BlockSpec(a.shape, lambda *_: (0,) * a.ndim)
    sds = jax.ShapeDtypeStruct
    ins = [x.reshape(rows, d), og.reshape(rows, -1), osb.reshape(rows, -1), wo, nw.reshape(1, d), wcq]
    x2, q = pl.pallas_call(
        _mix_query_kernel, grid=(1,), in_specs=[full(a) for a in ins],
        out_specs=[pl.BlockSpec((rows, d), lambda i: (0, 0))] * 2,
        out_shape=[sds((rows, d), F32)] * 2,
        compiler_params=_cparams(("arbitrary",)), name="mix_query_sample")(*ins)
    per_batch = pl.BlockSpec((1, r, d), lambda i: (i, 0, 0))
    mem = pl.BlockSpec((1,) + mk.shape[1:], lambda i: (i,) + (0,) * (mk.ndim - 1))
    o = pl.pallas_call(
        _xattn_core_kernel, grid=(b,), in_specs=[per_batch, mem, mem], out_specs=per_batch,
        out_shape=sds((b, r, d), F32),
        compiler_params=_cparams(("arbitrary",)), name="xattn_core_sample")(q.reshape(b, r, d), mk, mv)
    ins = [x2, o.reshape(rows, d), wco]
    y = pl.pallas_call(
        _xattn_out_kernel, grid=(1,), in_specs=[full(a) for a in ins],
        out_specs=pl.BlockSpec((rows, d), lambda i: (0, 0)),
        out_shape=sds((rows, d), F32),
        compiler_params=_cparams(("arbitrary",)), name="xattn_out_sample")(*ins)
    return y.reshape(b, r, d)


def _mix_xattn(x, og, osb, wo, nw, wcq, mk, mv, wco):
    b, r, d = x.shape
    tm = min(r, 1024)
    assert r % tm == 0
    ng = og.shape[2]
    ns = osb.shape[2]
    const = lambda shape: pl.BlockSpec(shape, lambda i, j: (0, 0), pipeline_mode=pl.Buffered(1))
    row = lambda n: pl.BlockSpec((1, tm, n), lambda i, j: (i, j, 0))
    mem = pl.BlockSpec((1,) + mk.shape[1:], lambda i, j: (i,) + (0,) * (mk.ndim - 1))
    return pl.pallas_call(
        _mix_xattn_kernel,
        grid=(b, r // tm),
        in_specs=[row(d), row(ng), row(ns), const((ng + ns, d)), const((1, d)), const((d, d)), mem, mem,
                  const((d, d))],
        out_specs=row(d),
        out_shape=jax.ShapeDtypeStruct((b, r, d), F32),
        compiler_params=_cparams(("arbitrary", "arbitrary")),
        name="mix_xattn",
    )(x, og, osb, wo, nw.reshape(1, d), wcq, mk, mv, wco)


def kernel(x_prompt, x_sample, mem_prompt, cache_sb_k, cache_sb_v, page_table, state_gla, cache_mem_k, cache_mem_v, ffn1_norm_w, ffn1_w_gate, ffn1_w_up, ffn1_w_down, mix_norm_w, w_in, w_gate2, b_gate2, gla_norm_w, sb_bias, w_out, xattn_norm_w, mem_norm_w, w_mk, w_mv, w_cq, w_co, ffn2_norm_w, ffn2_w_gate, ffn2_w_up, ffn2_w_down, final_norm_w):
    depth = w_in.shape[0]
    b_p, t_p, d = x_prompt.shape
    b_d, t_d, _ = x_sample.shape
    assert b_p == 1
    n_pool, page = cache_sb_k.shape[1], cache_sb_k.shape[2]
    hd_s = SB_HEADS * SB_HD
    nv = GLA_HEADS * GLA_DV

    xp = x_prompt.reshape(b_p * t_p, d)
    xs = x_sample.reshape(b_d * t_d, d)
    outs = [[] for _ in range(8)]
    for l in range(depth):
        last = l == depth - 1
        f1 = _ffn_weights(ffn1_w_gate[l], ffn1_w_up[l], ffn1_w_down[l])
        f2 = _ffn_weights(ffn2_w_gate[l], ffn2_w_up[l], ffn2_w_down[l])
        wproj_p, wproj_s = _prep_inproj_weights(w_in[l], w_gate2[l], b_gate2[l])
        wo = w_out[l].astype(BF16)
        wcq = w_cq[l].astype(BF16)
        wco = w_co[l].astype(BF16)

        mk_p, mv_p = _memory_kv(mem_prompt, mem_norm_w[l], w_mk[l], w_mv[l])

        x1 = _ffn(xp, ffn1_norm_w[l], *f1, final_norm_w, False)
        qg, vg, rg, la, qs, kgt, lat, kst, vst, kstb, vstb = _inproj_prompt(x1, mix_norm_w[l], wproj_p)
        og, s_p = _gla_prompt(qg, vg, rg, la, kgt, lat, gla_norm_w[l])
        osb = _sb_prompt(qs, kstb, vstb, sb_bias[l])
        x3 = _mix_xattn(x1.reshape(b_p, t_p, d), og.reshape(b_p, t_p, nv), osb.reshape(b_p, t_p, hd_s),
                        wo, xattn_norm_w[l], wcq, mk_p, mv_p, wco)
        xp = _ffn(x3.reshape(b_p * t_p, d), ffn2_norm_w[l], *f2, final_norm_w, last)
        seq_major = lambda a: a.reshape(b_p, SB_HEADS, SB_HD, t_p).transpose(0, 3, 1, 2)
        outs[0].append(seq_major(kst))
        outs[1].append(seq_major(vst))
        outs[2].append(s_p.reshape(b_p, GLA_HEADS, GLA_DK, GLA_DV))
        outs[3].append(mk_p.reshape(b_p, -1, X_HEADS, d // X_HEADS))
        outs[4].append(mv_p.reshape(b_p, -1, X_HEADS, d // X_HEADS))

        x1 = _ffn(xs, ffn1_norm_w[l], *f1, final_norm_w, False)
        vg, rg, qs, ks, vs, qgt, kgt, lat = _inproj_sample(x1, mix_norm_w[l], wproj_s)
        tok_major = lambda a: a.reshape(a.shape[0], b_d, t_d).transpose(1, 0, 2)
        feat_major = lambda c: c.transpose(0, 2, 3, 1).reshape(n_pool, hd_s, page)
        og, s_s = _gla_sample(tok_major(qgt), tok_major(kgt), tok_major(lat),
                              vg.reshape(b_d, t_d, nv), rg.reshape(b_d, t_d, nv), state_gla[l], gla_norm_w[l])
        osb = _sb_sample(qs.reshape(b_d, t_d, hd_s), ks.reshape(b_d, t_d, hd_s), vs.reshape(b_d, t_d, hd_s),
                         feat_major(cache_sb_k[l]), feat_major(cache_sb_v[l]), page_table, sb_bias[l])
        x3 = _mix_xattn_sample(x1.reshape(b_d, t_d, d), og, osb, wo, xattn_norm_w[l], wcq,
                               cache_mem_k[l], cache_mem_v[l], wco)
        xs = _ffn(x3.reshape(b_d * t_d, d), ffn2_norm_w[l], *f2, final_norm_w, last)
        outs[5].append(ks.reshape(b_d, t_d, SB_HEADS, SB_HD))
        outs[6].append(vs.reshape(b_d, t_d, SB_HEADS, SB_HD))
        outs[7].append(s_s)

    stk = [jnp.stack(o) for o in outs]
    return (xp.reshape(b_p, t_p, d), xs.reshape(b_d, t_d, d),
            stk[0], stk[1], stk[2], stk[3], stk[4], stk[5], stk[6], stk[7])
```

```python
import functools

import jax
import jax.numpy as jnp
from jax import lax
from jax.experimental import pallas as pl
from jax.experimental.pallas import tpu as pltpu

F32 = jnp.float32
BF16 = jnp.bfloat16

NORM_EPS = 1e-6
FFN_RES = 0.5
GLA_HEADS = 4
GLA_DK = 64
GLA_DV = 128
GLA_GATE_RANK = 16
GLA_GATE_TAU = 16.0
GLA_CHUNK = 64
GLA_TILE = 512
SB_HEADS = 8
SB_HD = 64
SB_MASKED_LOGIT = -1e30
SB_AUG = 16
SB_AUG_ONES = 3
SB_QUERY_ROWS = 256
SB_STEPS_PER_TRIP = 6
SB_HEADS_PER_STEP = 4
SB_PAGES_PER_STEP = 16
LOG2E = 1.4426950408889634
X_HEADS = 4
LANE = 128
MXU_DIM = 256
VMEM_LIMIT = 56 * 1024 * 1024


def _cparams(sem):
    return pltpu.CompilerParams(dimension_semantics=sem, vmem_limit_bytes=VMEM_LIMIT)


def _rms(x, w):
    return x * lax.rsqrt(jnp.mean(x * x, axis=-1, keepdims=True) + NORM_EPS) * w


def _softplus(z):
    return jnp.maximum(z, 0.0) + jnp.log(1.0 + jnp.exp(-jnp.abs(z)))


def _log_sigmoid(z):
    return -_softplus(-z)


def _sigmoid(z):
    return 1.0 / (1.0 + jnp.exp(-z))


def _dot(a, b):
    return jnp.dot(a, b, preferred_element_type=F32)


def _dot_nt(a, b):
    return lax.dot_general(a, b, (((1,), (1,)), ((), ())), preferred_element_type=F32)


def _split_bf16(x):
    hi = x.astype(BF16)
    lo = (x - hi.astype(F32)).astype(BF16)
    return hi, lo


def _memkv_kernel(mem_ref, nw_ref, wk_ref, wv_ref, k_ref, v_ref):
    mn = _rms(mem_ref[0], nw_ref[...]).astype(BF16)
    k_ref[0] = _dot(mn, wk_ref[...])
    v_ref[0] = _dot(mn, wv_ref[...])


def _memory_kv(mem, nw, wk, wv):
    b, m, d = mem.shape
    full = lambda shape: pl.BlockSpec(shape, lambda i: (0,) * len(shape))
    row = pl.BlockSpec((1, m, d), lambda i: (i, 0, 0))
    return pl.pallas_call(
        _memkv_kernel,
        grid=(b,),
        in_specs=[row, full((1, d)), full((d, d)), full((d, d))],
        out_specs=[row, row],
        out_shape=[jax.ShapeDtypeStruct((b, m, d), F32)] * 2,
        compiler_params=_cparams(("arbitrary",)),
        name="memory_kv",
    )(mem, nw.reshape(1, d), wk.astype(BF16), wv.astype(BF16))


def _ffn_kernel(x_ref, nw_ref, wg_ref, wu_ref, wd_ref, fw_ref, o_ref, *, final_norm):
    x = x_ref[...]
    h = _rms(x, nw_ref[...]).astype(BF16)
    acc = jnp.zeros_like(x)
    ff = wg_ref.shape[0]
    for c0 in range(0, ff, MXU_DIM):
        sl = slice(c0, min(c0 + MXU_DIM, ff))
        g = _dot_nt(h, wg_ref[sl, :])
        u = _dot_nt(h, wu_ref[sl, :])
        a = (g * _sigmoid(g) * u).astype(BF16)
        acc = acc + _dot(a, wd_ref[sl, :])
    y = x + FFN_RES * acc
    if final_norm:
        y = _rms(y, fw_ref[...])
    o_ref[...] = y


def _ffn_weights(w_gate, w_up, w_down):
    return w_gate.T.astype(BF16), w_up.T.astype(BF16), w_down.astype(BF16)


def _ffn(x, nw, wg, wu, wd, fw, final_norm):
    rows, d = x.shape
    tm = min(rows, 512)
    assert rows % tm == 0
    ff = wg.shape[0]
    const = lambda shape: pl.BlockSpec(shape, lambda i: (0, 0), pipeline_mode=pl.Buffered(1))
    row = pl.BlockSpec((tm, d), lambda i: (i, 0))
    return pl.pallas_call(
        functools.partial(_ffn_kernel, final_norm=final_norm),
        grid=(rows // tm,),
        in_specs=[row, const((1, d)), const((ff, d)), const((ff, d)), const((ff, d)), const((1, d))],
        out_specs=row,
        out_shape=jax.ShapeDtypeStruct((rows, d), F32),
        compiler_params=_cparams(("arbitrary",)),
        name="ffn_final" if final_norm else "ffn",
    )(x, nw.reshape(1, d), wg, wu, wd, fw.reshape(1, d))


NQK = GLA_HEADS * GLA_DK
NV = GLA_HEADS * GLA_DV
NS = SB_HEADS * SB_HD


def _gate_log_decay(pre):
    return _log_sigmoid(pre) * (1.0 / GLA_GATE_TAU)


def _inproj_prompt_kernel(x_ref, nw_ref, wn_ref, wt_ref, wglrt_ref, wg2_ref, bg2_ref, wg2t_ref, bg2t_ref,
                          qg_ref, vg_ref, rg_ref, la_ref, qs_ref,
                          kgt_ref, lat_ref, kst_ref, vst_ref, kstb_ref, vstb_ref):
    h = _rms(x_ref[...], nw_ref[...]).astype(BF16)
    y = _dot(h, wn_ref[...])
    qg_ref[...] = y[:, 0:NQK] * (GLA_DK ** -0.5)
    vg_ref[...] = y[:, NQK:NQK + NV]
    rg_ref[...] = y[:, NQK + NV:NQK + 2 * NV]
    o = NQK + 2 * NV
    qs_ref[...] = (y[:, o:o + NS] * (SB_HD ** -0.5 * LOG2E)).astype(BF16)
    glr = y[:, o + NS:o + NS + LANE].astype(BF16)
    la_ref[...] = _gate_log_decay(_dot(glr, wg2_ref[...]) + bg2_ref[...])
    yt = _dot_nt(wt_ref[...], h)
    kgt_ref[...] = yt[0:NQK]
    kst = yt[NQK:NQK + NS]
    vst = yt[NQK + NS:NQK + 2 * NS]
    kst_ref[...] = kst
    vst_ref[...] = vst
    aug_row = lax.broadcasted_iota(jnp.int32, (SB_AUG, kst.shape[1]), 0)
    ones_rows = jnp.where(aug_row < SB_AUG_ONES, 1.0, 0.0).astype(BF16)
    for hh in range(SB_HEADS):
        base = hh * (SB_HD + SB_AUG)
        kstb_ref[base:base + SB_HD, :] = kst[hh * SB_HD:(hh + 1) * SB_HD].astype(BF16)
        kstb_ref[base + SB_HD:base + SB_HD + SB_AUG, :] = ones_rows
    vstb_ref[...] = vst.astype(BF16)
    glrt = _dot_nt(wglrt_ref[...], h).astype(BF16)
    lat_ref[...] = _gate_log_decay(_dot(wg2t_ref[...], glrt) + bg2t_ref[...])


def _inproj_sample_kernel(x_ref, nw_ref, wn_ref, wt_ref, wglrt_ref, wg2t_ref, bg2t_ref,
                          vg_ref, rg_ref, qs_ref, ks_ref, vs_ref, qgt_ref, kgt_ref, lat_ref):
    h = _rms(x_ref[...], nw_ref[...]).astype(BF16)
    y = _dot(h, wn_ref[...])
    vg_ref[...] = y[:, 0:NV]
    rg_ref[...] = y[:, NV:2 * NV]
    o = 2 * NV
    qs_ref[...] = (y[:, o:o + NS] * (SB_HD ** -0.5 * LOG2E)).astype(BF16)
    ks_ref[...] = y[:, o + NS:o + 2 * NS]
    vs_ref[...] = y[:, o + 2 * NS:o + 3 * NS]
    yt = _dot_nt(wt_ref[...], h)
    qgt_ref[...] = yt[0:NQK] * (GLA_DK ** -0.5)
    kgt_ref[...] = yt[NQK:2 * NQK]
    glrt = _dot_nt(wglrt_ref[...], h).astype(BF16)
    lat_ref[...] = _gate_log_decay(_dot(wg2t_ref[...], glrt) + bg2t_ref[...])


def _prep_inproj_weights(w_in, w_gate2, b_gate2):
    sizes = (NQK, NQK, NV, GLA_GATE_RANK, NV, NS, NS, NS)
    offs = [0]
    for s in sizes:
        offs.append(offs[-1] + s)
    qg, kg, vg, glr, rg, qs, ks, vs = [w_in[:, offs[i]:offs[i + 1]] for i in range(8)]
    glr_pad = jnp.pad(glr, ((0, 0), (0, LANE - GLA_GATE_RANK)))
    cat = lambda parts: jnp.concatenate(parts, axis=1)
    shared = dict(
        wglrt=glr.T.astype(BF16),
        wg2=jnp.pad(w_gate2, ((0, LANE - GLA_GATE_RANK), (0, 0))).astype(BF16),
        bg2=b_gate2.reshape(1, NQK),
        wg2t=w_gate2.T.astype(BF16),
        bg2t=b_gate2.reshape(NQK, 1))
    prompt = dict(wn=cat([qg, vg, rg, qs, glr_pad]).astype(BF16), wt=cat([kg, ks, vs]).T.astype(BF16), **shared)
    sample = dict(wn=cat([vg, rg, qs, ks, vs]).astype(BF16), wt=cat([qg, kg]).T.astype(BF16), **shared)
    return prompt, sample


def _inproj_call(kernel_fn, name, x, inputs, row_outs, col_outs):
    rows, d = x.shape
    tm = min(rows, 512)
    assert rows % tm == 0
    const = lambda a: pl.BlockSpec(a.shape, lambda i: (0, 0), pipeline_mode=pl.Buffered(1))
    sds = jax.ShapeDtypeStruct
    return pl.pallas_call(
        kernel_fn,
        grid=(rows // tm,),
        in_specs=[pl.BlockSpec((tm, d), lambda i: (i, 0))] + [const(a) for a in inputs],
        out_specs=[pl.BlockSpec((tm, n), lambda i: (i, 0)) for n, _ in row_outs]
                  + [pl.BlockSpec((n, tm), lambda i: (0, i)) for n, _ in col_outs],
        out_shape=[sds((rows, n), dt) for n, dt in row_outs] + [sds((n, rows), dt) for n, dt in col_outs],
        compiler_params=_cparams(("arbitrary",)),
        name=name,
    )(x, *inputs)


def _inproj_prompt(x, nw, w):
    inputs = [nw.reshape(1, -1), w["wn"], w["wt"], w["wglrt"], w["wg2"], w["bg2"], w["wg2t"], w["bg2t"]]
    return _inproj_call(_inproj_prompt_kernel, "in_proj_prompt", x, inputs,
                        [(NQK, F32), (NV, F32), (NV, F32), (NQK, F32), (NS, BF16)],
                        [(NQK, F32), (NQK, F32), (NS, F32), (NS, F32), (SB_HEADS * (SB_HD + SB_AUG), BF16),
                         (NS, BF16)])


def _inproj_sample(x, nw, w):
    inputs = [nw.reshape(1, -1), w["wn"], w["wt"], w["wglrt"], w["wg2t"], w["bg2t"]]
    return _inproj_call(_inproj_sample_kernel, "in_proj_sample", x, inputs,
                        [(NV, F32), (NV, F32), (NS, BF16), (NS, F32), (NS, F32)],
                        [(NQK, F32), (NQK, F32), (NQK, F32)])


def _gla_out(o, gnw, r):
    o = o * lax.rsqrt(jnp.mean(o * o, axis=-1, keepdims=True) + NORM_EPS) * gnw
    return o * (r * _sigmoid(r))


def _gla_prompt_kernel(q_ref, v_ref, r_ref, la_ref, kt_ref, lat_ref, gnw_ref, o_ref, s_out_ref, s_ref):
    i = pl.program_id(0)
    c = GLA_CHUNK

    @pl.when(i == 0)
    def _():
        s_ref[...] = jnp.zeros_like(s_ref)

    rr = lax.broadcasted_iota(jnp.int32, (c, c), 0)
    cc = lax.broadcasted_iota(jnp.int32, (c, c), 1)
    causal = cc <= rr
    tri = jnp.where(causal, 1.0, 0.0).astype(BF16)
    trit = jnp.where(rr <= cc, 1.0, 0.0).astype(BF16)
    mid = c // 2
    state = [s_ref[h] for h in range(GLA_HEADS)]

    for ch in range(GLA_TILE // c):
        rows = slice(ch * c, (ch + 1) * c)
        la_hi, la_lo = _split_bf16(la_ref[rows, :])
        b = _dot(tri, la_hi) + _dot(tri, la_lo)
        lat_hi, lat_lo = _split_bf16(lat_ref[:, rows])
        bt = _dot(lat_hi, trit) + _dot(lat_lo, trit)
        q = q_ref[rows, :]
        kt = kt_ref[:, rows]
        qe = q * jnp.exp(b)
        qm = q * jnp.exp(b - b[mid:mid + 1, :])
        ktm = kt * jnp.exp(bt[:, mid:mid + 1] - bt)
        ktd = kt * jnp.exp(bt[:, c - 1:c] - bt)
        dec = jnp.exp(bt[:, c - 1:c])
        for h in range(GLA_HEADS):
            ks = slice(h * GLA_DK, (h + 1) * GLA_DK)
            vs = slice(h * GLA_DV, (h + 1) * GLA_DV)
            s = state[h]
            v = v_ref[rows, vs]
            att = jnp.where(causal, _dot(qm[:, ks], ktm[ks, :]), 0.0)
            o = _dot(qe[:, ks], s) + _dot(att, v)
            state[h] = dec[ks, :] * s + _dot(ktd[ks, :], v)
            o_ref[rows, vs] = _gla_out(o, gnw_ref[h:h + 1, :], r_ref[rows, vs])

    for h in range(GLA_HEADS):
        s_ref[h] = state[h]

    @pl.when(i == pl.num_programs(0) - 1)
    def _():
        s_out_ref[...] = s_ref[...]


def _gla_prompt(qg, vg, rg, la, kgt, lat, gnw):
    t = qg.shape[0]
    nqk = GLA_HEADS * GLA_DK
    nv = GLA_HEADS * GLA_DV
    tg = GLA_TILE
    assert t % tg == 0
    row = lambda n: pl.BlockSpec((tg, n), lambda i: (i, 0))
    col = lambda n: pl.BlockSpec((n, tg), lambda i: (0, i))
    return pl.pallas_call(
        _gla_prompt_kernel,
        grid=(t // tg,),
        in_specs=[row(nqk), row(nv), row(nv), row(nqk), col(nqk), col(nqk),
                  pl.BlockSpec((GLA_HEADS, GLA_DV), lambda i: (0, 0))],
        out_specs=[row(nv), pl.BlockSpec((GLA_HEADS, GLA_DK, GLA_DV), lambda i: (0, 0, 0))],
        out_shape=[jax.ShapeDtypeStruct((t, nv), F32),
                   jax.ShapeDtypeStruct((GLA_HEADS, GLA_DK, GLA_DV), F32)],
        scratch_shapes=[pltpu.VMEM((GLA_HEADS, GLA_DK, GLA_DV), F32)],
        compiler_params=_cparams(("arbitrary",)),
        name="gla_prompt",
    )(qg, vg, rg, la, kgt, lat, gnw)


def _gla_sample_kernel(qt_ref, kt_ref, lat_ref, v_ref, r_ref, s0_ref, gnw_ref, o_ref, s_out_ref):
    n_t = v_ref.shape[1]
    for h in range(GLA_HEADS):
        ks = slice(h * GLA_DK, (h + 1) * GLA_DK)
        vs = slice(h * GLA_DV, (h + 1) * GLA_DV)
        s = s0_ref[0, h]
        for t in range(n_t):
            a = jnp.exp(lat_ref[0, ks, t:t + 1])
            s = a * s + kt_ref[0, ks, t:t + 1] * v_ref[0, t:t + 1, vs]
            o = jnp.sum(qt_ref[0, ks, t:t + 1] * s, axis=0, keepdims=True)
            o_ref[0, t:t + 1, vs] = _gla_out(o, gnw_ref[h:h + 1, :], r_ref[0, t:t + 1, vs])
        s_out_ref[0, h] = s


def _gla_sample(qgt, kgt, lat, vg, rg, s0, gnw):
    b, n_t, nv = vg.shape
    nqk = GLA_HEADS * GLA_DK
    colb = pl.BlockSpec((1, nqk, n_t), lambda i: (i, 0, 0))
    rowb = pl.BlockSpec((1, n_t, nv), lambda i: (i, 0, 0))
    st = pl.BlockSpec((1, GLA_HEADS, GLA_DK, GLA_DV), lambda i: (i, 0, 0, 0))
    return pl.pallas_call(
        _gla_sample_kernel,
        grid=(b,),
        in_specs=[colb, colb, colb, rowb, rowb, st, pl.BlockSpec((GLA_HEADS, GLA_DV), lambda i: (0, 0))],
        out_specs=[rowb, st],
        out_shape=[jax.ShapeDtypeStruct((b, n_t, nv), F32),
                   jax.ShapeDtypeStruct((b, GLA_HEADS, GLA_DK, GLA_DV), F32)],
        compiler_params=_cparams(("arbitrary",)),
        name="gla_sample",
    )(qgt, kgt, lat, vg, rg, s0, gnw)


def _neg_tri(n, dtype):
    rr = lax.broadcasted_iota(jnp.int32, (n, n), 0)
    cc = lax.broadcasted_iota(jnp.int32, (n, n), 1)
    return jnp.where(rr >= cc, -1.0, 0.0).astype(dtype)


def _softplus2(z2):
    sign_bit = jnp.uint32(0x80000000)
    neg_abs = pltpu.bitcast(pltpu.bitcast(z2, jnp.uint32) | sign_bit, F32)
    return jnp.maximum(z2, 0.0) + jnp.log2(1.0 + jnp.exp2(neg_abs))


def _sb_prompt_kernel(bias_ref, q_ref, kt_ref, vt_ref, o_ref, q_sc, z_sc, sp_sc, w_sc, scale_sc, *, qr, blk, hp):
    g = pl.program_id(0)
    i = pl.program_id(1)
    n_diag = qr // blk
    top = (i + 1) * n_diag - 1
    ntri = _neg_tri(blk, BF16)
    aug_col = lax.broadcasted_iota(jnp.int32, (qr, SB_AUG), 1)
    for hh in range(hp):
        b2 = jnp.full((qr, SB_AUG), bias_ref[g * hp + hh] * LOG2E, F32)
        b_hi = b2.astype(BF16).astype(F32)
        bias_cols = jnp.where(aug_col == 0, b_hi, jnp.where(aug_col == 1, b2 - b_hi, 0.0))
        for masked in range(2):
            q_sc[masked, hh, :, 0:SB_HD] = q_ref[:, hh * SB_HD:(hh + 1) * SB_HD]
            mask_val = SB_MASKED_LOGIT if masked else 0.0
            q_sc[masked, hh, :, SB_HD:SB_HD + SB_AUG] = jnp.where(aug_col == 2, mask_val, bias_cols).astype(BF16)
        w_sc[0, hh] = jnp.zeros((qr, blk), BF16)
        scale_sc[0, hh] = jnp.ones((1, qr), F32)

    def key_block(ref, hh, j):
        return ref[hh, :, pl.ds(pl.multiple_of(j * blk, blk), blk)]

    rr = lax.broadcasted_iota(jnp.int32, (qr, blk), 0)
    cc = lax.broadcasted_iota(jnp.int32, (qr, blk), 1)

    def logits_stage(n, slot, hh, diagonal):
        j = jnp.maximum(top - n, 0)
        masked = jnp.where(n <= top, 0, 1)
        z = _dot(q_sc[masked, hh], key_block(kt_ref, hh, j))
        sp = _softplus2(z)
        if diagonal:
            keep = cc + (n_diag - 1 - n) * blk < rr
            z = jnp.where(keep, z, SB_MASKED_LOGIT)
            sp = jnp.where(keep, sp, 0.0)
        z_sc[slot, hh] = z
        sp_sc[slot, hh] = sp.astype(BF16)

    def step(n, slot, st, diagonal=False):
        prev = 1 - slot
        j_value = jnp.clip(top - n + 2, 0, top)
        out = []
        for hh in range(hp):
            carry, acc = st[2 * hh], st[2 * hh + 1]
            acc = acc + _dot_nt(key_block(vt_ref, hh, j_value), w_sc[prev, hh]) * scale_sc[prev, hh]
            lsum = _dot(sp_sc[prev, hh], ntri)
            w_sc[slot, hh] = jnp.exp2(z_sc[prev, hh] + lsum).astype(BF16)
            scale_sc[slot, hh] = jnp.exp2(carry)
            carry = carry + lsum[:, 0:LANE].T[0:1, :]
            logits_stage(n, slot, hh, diagonal)
            out.extend((carry, acc))
        return tuple(out)

    state = (jnp.zeros((1, qr), F32), jnp.zeros((SB_HD, qr), F32)) * hp
    for hh in range(hp):
        logits_stage(0, 0, hh, True)
    for n in range(1, n_diag):
        state = step(n, n % 2, state, True)

    spt = SB_STEPS_PER_TRIP
    first = n_diag

    def body(p, st):
        for u in range(spt):
            st = step(spt * p + first + u, (first + u) % 2, st)
        return st

    def pair(n0, st):
        return step(n0 + 1, (first + 1) % 2, step(n0, first % 2, st))

    n_steps = top + 3 - first
    n_trips = n_steps // spt
    state = lax.fori_loop(0, n_trips, body, state)
    for k in range(spt // 2):
        n0 = first + spt * n_trips + 2 * k
        state = lax.cond(n0 <= top + 2, functools.partial(pair, n0), lambda st: st, state)
    for hh in range(hp):
        o_ref[:, hh * SB_HD:(hh + 1) * SB_HD] = state[2 * hh + 1].T


def _sb_prompt(qs, kst, vst, bias):
    t = qs.shape[0]
    blk = min(t, MXU_DIM)
    qr = min(t, SB_QUERY_ROWS)
    assert t % qr == 0 and qr % blk == 0
    hp = SB_HEADS_PER_STEP
    dk = SB_HD + SB_AUG
    whole_seq = lambda rows_per_head: pl.BlockSpec((hp, rows_per_head, t), lambda h, i: (h, 0, 0),
                                                   pipeline_mode=pl.Buffered(1))
    rows = pl.BlockSpec((qr, hp * SB_HD), lambda h, i: (i, h))
    return pl.pallas_call(
        functools.partial(_sb_prompt_kernel, qr=qr, blk=blk, hp=hp),
        grid=(SB_HEADS // hp, t // qr),
        in_specs=[pl.BlockSpec(memory_space=pltpu.SMEM), rows, whole_seq(dk), whole_seq(SB_HD)],
        out_specs=rows,
        out_shape=jax.ShapeDtypeStruct((t, SB_HEADS * SB_HD), F32),
        scratch_shapes=[pltpu.VMEM((2, hp, qr, dk), BF16), pltpu.VMEM((2, hp, qr, blk), F32),
                        pltpu.VMEM((2, hp, qr, blk), BF16), pltpu.VMEM((2, hp, qr, blk), BF16),
                        pltpu.VMEM((2, hp, 1, qr), F32)],
        compiler_params=_cparams(("arbitrary", "arbitrary")),
        name="sb_prompt",
    )(bias, qs, kst.reshape(SB_HEADS, dk, t), vst.reshape(SB_HEADS, SB_HD, t))


def _sb_sample_kernel(pt_ref, q_ref, bias_ref, kn_ref, vn_ref, *rest, n_t, n_pg):
    k_refs = rest[:n_pg]
    v_refs = rest[n_pg:2 * n_pg]
    o_ref, carry_ref, acc_ref = rest[2 * n_pg:]
    p = pl.program_id(1)
    q = q_ref[0]
    nr, hd = q.shape
    bias = bias_ref[...]
    page = k_refs[0].shape[2]

    @pl.when(p == 0)
    def _():
        t_row = lax.broadcasted_iota(jnp.int32, (nr, 1), 0) // SB_HEADS
        carry = jnp.zeros((nr, 1), F32)
        acc = jnp.zeros((nr, hd), F32)
        for s in range(n_t - 2, -1, -1):
            z = jnp.sum(q * kn_ref[0, s:s + 1, :], axis=-1, keepdims=True) + bias
            m = s < t_row
            carry = carry - jnp.where(m, _softplus2(z), 0.0)
            w = jnp.where(m, jnp.exp2(z + carry), 0.0)
            acc = acc + w * vn_ref[0, s:s + 1, :]
        carry_ref[...] = carry
        acc_ref[...] = acc

    qb = q.astype(BF16)
    ntri = _neg_tri(page, BF16)
    zs = [_dot(qb, k_refs[g][0].astype(BF16)) + bias for g in range(n_pg)]
    ls = [_dot(_softplus2(z).astype(BF16), ntri) for z in zs]
    carry = carry_ref[...]
    acc = jnp.zeros((nr, hd), F32)
    for g in range(n_pg):
        w = jnp.exp2(zs[g] + ls[g] + carry)
        acc = acc + _dot_nt(w.astype(BF16), v_refs[g][0].astype(BF16))
        carry = carry + ls[g][:, 0:1]
    carry_ref[...] = carry
    acc_ref[...] += acc

    @pl.when(p == pl.num_programs(1) - 1)
    def _():
        row_head = lax.broadcasted_iota(jnp.int32, (SB_HEADS, hd), 0)
        lane_head = lax.broadcasted_iota(jnp.int32, (SB_HEADS, hd), 1) // SB_HD
        own = (row_head == lane_head)[None]
        a3 = acc_ref[...].reshape(n_t, SB_HEADS, hd)
        o_ref[0] = jnp.sum(jnp.where(own, a3, 0.0), axis=1)


def _sb_sample(qs, ks_new, vs_new, pool_kt, pool_vt, page_table, bias):
    b, n_t, hd = qs.shape
    n_pages = page_table.shape[1]
    page = pool_kt.shape[2]
    n_pg = SB_PAGES_PER_STEP
    while n_pages % n_pg:
        n_pg //= 2
    nr = SB_HEADS * n_t
    head_of_lane = jnp.arange(hd, dtype=jnp.int32) // SB_HD
    head_of_row = jnp.arange(nr, dtype=jnp.int32) % SB_HEADS
    sel = (head_of_row[:, None] == head_of_lane[None, :])
    q_rows = jnp.repeat(qs.astype(F32), SB_HEADS, axis=1)
    q_bd = jnp.where(sel[None], q_rows, 0.0)
    bias_col = (bias[head_of_row] * LOG2E).reshape(nr, 1)

    def page_spec(g):
        return pl.BlockSpec((1, hd, page), lambda i, p, pt: (pt[i, n_pages - 1 - (p * n_pg + g)], 0, 0))

    grid_spec = pltpu.PrefetchScalarGridSpec(
        num_scalar_prefetch=1,
        grid=(b, n_pages // n_pg),
        in_specs=[pl.BlockSpec((1, nr, hd), lambda i, p, pt: (i, 0, 0)),
                  pl.BlockSpec((nr, 1), lambda i, p, pt: (0, 0)),
                  pl.BlockSpec((1, n_t, hd), lambda i, p, pt: (i, 0, 0)),
                  pl.BlockSpec((1, n_t, hd), lambda i, p, pt: (i, 0, 0))]
                 + [page_spec(g) for g in range(n_pg)] * 2,
        out_specs=pl.BlockSpec((1, n_t, hd), lambda i, p, pt: (i, 0, 0)),
        scratch_shapes=[pltpu.VMEM((nr, 1), F32), pltpu.VMEM((nr, hd), F32)],
    )
    return pl.pallas_call(
        functools.partial(_sb_sample_kernel, n_t=n_t, n_pg=n_pg),
        grid_spec=grid_spec,
        out_shape=jax.ShapeDtypeStruct((b, n_t, hd), F32),
        compiler_params=_cparams(("arbitrary", "arbitrary")),
        name="sb_sample",
    )(page_table, q_bd, bias_col, ks_new, vs_new, *([pool_kt] * n_pg), *([pool_vt] * n_pg))


def _mix_residual(x, og, osb, wo_ref):
    ng = og.shape[1]
    return x + _dot(og.astype(BF16), wo_ref[0:ng, :]) + _dot(osb.astype(BF16), wo_ref[ng:, :])


def _xattn_query(x, nw_ref, wcq_ref):
    h = _rms(x, nw_ref[...]).astype(BF16)
    return _dot(h, wcq_ref[...]) * ((x.shape[1] // X_HEADS) ** -0.5)


def _xattn_heads(q, mk_ref, mv_ref):
    hd = q.shape[1] // X_HEADS
    outs = []
    for hh in range(X_HEADS):
        sl = slice(hh * hd, (hh + 1) * hd)
        mk = mk_ref[0, :, sl] if len(mk_ref.shape) == 3 else mk_ref[0, :, hh, :]
        mv = mv_ref[0, :, sl] if len(mv_ref.shape) == 3 else mv_ref[0, :, hh, :]
        s = _dot_nt(q[:, sl], mk)
        s = s - jnp.max(s, axis=-1, keepdims=True)
        e = jnp.exp(s)
        p = e / jnp.sum(e, axis=-1, keepdims=True)
        outs.append(_dot(p, mv))
    return jnp.concatenate(outs, axis=-1)


def _mix_xattn_kernel(x_ref, og_ref, os_ref, wo_ref, nw_ref, wcq_ref, mk_ref, mv_ref, wco_ref, o_ref):
    x = _mix_residual(x_ref[0], og_ref[0], os_ref[0], wo_ref)
    o = _xattn_heads(_xattn_query(x, nw_ref, wcq_ref), mk_ref, mv_ref)
    o_ref[0] = x + _dot(o.astype(BF16), wco_ref[...])


def _mix_query_kernel(x_ref, og_ref, os_ref, wo_ref, nw_ref, wcq_ref, x2_ref, q_ref):
    x = _mix_residual(x_ref[...], og_ref[...], os_ref[...], wo_ref)
    x2_ref[...] = x
    q_ref[...] = _xattn_query(x, nw_ref, wcq_ref)


def _xattn_core_kernel(q_ref, mk_ref, mv_ref, o_ref):
    o_ref[0] = _xattn_heads(q_ref[0], mk_ref, mv_ref)


def _xattn_out_kernel(x_ref, o_ref, wco_ref, y_ref):
    y_ref[...] = x_ref[...] + _dot(o_ref[...].astype(BF16), wco_ref[...])


def _mix_xattn_sample(x, og, osb, wo, nw, wcq, mk, mv, wco):
    b, r, d = x.shape
    rows = b * r
    full = lambda a: pl.BlockSpec(a.shape, lambda *_: (0,) * a.ndim)
    sds = jax.ShapeDtypeStruct
    ins = [x.reshape(rows, d), og.reshape(rows, -1), osb.reshape(rows, -1), wo, nw.reshape(1, d), wcq]
    x2, q = pl.pallas_call(
        _mix_query_kernel, grid=(1,), in_specs=[full(a) for a in ins],
        out_specs=[pl.BlockSpec((rows, d), lambda i: (0, 0))] * 2,
        out_shape=[sds((rows, d), F32)] * 2,
        compiler_params=_cparams(("arbitrary",)), name="mix_query_sample")(*ins)
    per_batch = pl.BlockSpec((1, r, d), lambda i: (i, 0, 0))
    mem = pl.BlockSpec((1,) + mk.shape[1:], lambda i: (i,) + (0,) * (mk.ndim - 1))
    o = pl.pallas_call(
        _xattn_core_kernel, grid=(b,), in_specs=[per_batch, mem, mem], out_specs=per_batch,
        out_shape=sds((b, r, d), F32),
        compiler_params=_cparams(("arbitrary",)), name="xattn_core_sample")(q.reshape(b, r, d), mk, mv)
    ins = [x2, o.reshape(rows, d), wco]
    y = pl.pallas_call(
        _xattn_out_kernel, grid=(1,), in_specs=[full(a) for a in ins],
        out_specs=pl.BlockSpec((rows, d), lambda i: (0, 0)),
        out_shape=sds((rows, d), F32),
        compiler_params=_cparams(("arbitrary",)), name="xattn_out_sample")(*ins)
    return y.reshape(b, r, d)


def _mix_xattn(x, og, osb, wo, nw, wcq, mk, mv, wco):
    b, r, d = x.shape
    tm = min(r, 1024)
    assert r % tm == 0
    ng = og.shape[2]
    ns = osb.shape[2]
    const = lambda shape: pl.BlockSpec(shape, lambda i, j: (0, 0), pipeline_mode=pl.Buffered(1))
    row = lambda n: pl.BlockSpec((1, tm, n), lambda i, j: (i, j, 0))
    mem = pl.BlockSpec((1,) + mk.shape[1:], lambda i, j: (i,) + (0,) * (mk.ndim - 1))
    return pl.pallas_call(
        _mix_xattn_kernel,
        grid=(b, r // tm),
        in_specs=[row(d), row(ng), row(ns), const((ng + ns, d)), const((1, d)), const((d, d)), mem, mem,
                  const((d, d))],
        out_specs=row(d),
        out_shape=jax.ShapeDtypeStruct((b, r, d), F32),
        compiler_params=_cparams(("arbitrary", "arbitrary")),
        name="mix_xattn",
    )(x, og, osb, wo, nw.reshape(1, d), wcq, mk, mv, wco)


def kernel(x_prompt, x_sample, mem_prompt, cache_sb_k, cache_sb_v, page_table, state_gla, cache_mem_k, cache_mem_v, ffn1_norm_w, ffn1_w_gate, ffn1_w_up, ffn1_w_down, mix_norm_w, w_in, w_gate2, b_gate2, gla_norm_w, sb_bias, w_out, xattn_norm_w, mem_norm_w, w_mk, w_mv, w_cq, w_co, ffn2_norm_w, ffn2_w_gate, ffn2_w_up, ffn2_w_down, final_norm_w):
    depth = w_in.shape[0]
    b_p, t_p, d = x_prompt.shape
    b_d, t_d, _ = x_sample.shape
    assert b_p == 1
    n_pool, page = cache_sb_k.shape[1], cache_sb_k.shape[2]
    hd_s = SB_HEADS * SB_HD
    nv = GLA_HEADS * GLA_DV

    xp = x_prompt.reshape(b_p * t_p, d)
    xs = x_sample.reshape(b_d * t_d, d)
    outs = [[] for _ in range(8)]
    for l in range(depth):
        last = l == depth - 1
        f1 = _ffn_weights(ffn1_w_gate[l], ffn1_w_up[l], ffn1_w_down[l])
        f2 = _ffn_weights(ffn2_w_gate[l], ffn2_w_up[l], ffn2_w_down[l])
        wproj_p, wproj_s = _prep_inproj_weights(w_in[l], w_gate2[l], b_gate2[l])
        wo = w_out[l].astype(BF16)
        wcq = w_cq[l].astype(BF16)
        wco = w_co[l].astype(BF16)

        mk_p, mv_p = _memory_kv(mem_prompt, mem_norm_w[l], w_mk[l], w_mv[l])

        x1 = _ffn(xp, ffn1_norm_w[l], *f1, final_norm_w, False)
        qg, vg, rg, la, qs, kgt, lat, kst, vst, kstb, vstb = _inproj_prompt(x1, mix_norm_w[l], wproj_p)
        og, s_p = _gla_prompt(qg, vg, rg, la, kgt, lat, gla_norm_w[l])
        osb = _sb_prompt(qs, kstb, vstb, sb_bias[l])
        x3 = _mix_xattn(x1.reshape(b_p, t_p, d), og.reshape(b_p, t_p, nv), osb.reshape(b_p, t_p, hd_s),
                        wo, xattn_norm_w[l], wcq, mk_p, mv_p, wco)
        xp = _ffn(x3.reshape(b_p * t_p, d), ffn2_norm_w[l], *f2, final_norm_w, last)
        seq_major = lambda a: a.reshape(b_p, SB_HEADS, SB_HD, t_p).transpose(0, 3, 1, 2)
        outs[0].append(seq_major(kst))
        outs[1].append(seq_major(vst))
        outs[2].append(s_p.reshape(b_p, GLA_HEADS, GLA_DK, GLA_DV))
        outs[3].append(mk_p.reshape(b_p, -1, X_HEADS, d // X_HEADS))
        outs[4].append(mv_p.reshape(b_p, -1, X_HEADS, d // X_HEADS))

        x1 = _ffn(xs, ffn1_norm_w[l], *f1, final_norm_w, False)
        vg, rg, qs, ks, vs, qgt, kgt, lat = _inproj_sample(x1, mix_norm_w[l], wproj_s)
        tok_major = lambda a: a.reshape(a.shape[0], b_d, t_d).transpose(1, 0, 2)
        feat_major = lambda c: c.transpose(0, 2, 3, 1).reshape(n_pool, hd_s, page)
        og, s_s = _gla_sample(tok_major(qgt), tok_major(kgt), tok_major(lat),
                              vg.reshape(b_d, t_d, nv), rg.reshape(b_d, t_d, nv), state_gla[l], gla_norm_w[l])
        osb = _sb_sample(qs.reshape(b_d, t_d, hd_s), ks.reshape(b_d, t_d, hd_s), vs.reshape(b_d, t_d, hd_s),
                         feat_major(cache_sb_k[l]), feat_major(cache_sb_v[l]), page_table, sb_bias[l])
        x3 = _mix_xattn_sample(x1.reshape(b_d, t_d, d), og, osb, wo, xattn_norm_w[l], wcq,
                               cache_mem_k[l], cache_mem_v[l], wco)
        xs = _ffn(x3.reshape(b_d * t_d, d), ffn2_norm_w[l], *f2, final_norm_w, last)
        outs[5].append(ks.reshape(b_d, t_d, SB_HEADS, SB_HD))
        outs[6].append(vs.reshape(b_d, t_d, SB_HEADS, SB_HD))
        outs[7].append(s_s)

    stk = [jnp.stack(o) for o in outs]
    return (xp.reshape(b_p, t_p, d), xs.reshape(b_d, t_d, d),
            stk[0], stk[1], stk[2], stk[3], stk[4], stk[5], stk[6], stk[7])
```
